```python
import jax, jax.numpy as jnp
from jax import lax
import numpy as np

D_MODEL = 1024
BATCH = 4
SEQ = 8192
DEPTH = 4

N_A_LAYERS = DEPTH // 2
N_B_LAYERS = DEPTH - N_A_LAYERS
CONV_WIDTH = 31
CONV_CH = D_MODEL
N_HEADS = 16
N_KV_GROUPS = 4
HEADS_PER_GROUP = N_HEADS // N_KV_GROUPS
HEAD_DIM = 64
CMP_LEN = 32
CMP_STRIDE = 16
SLC_LEN = 64
SLC_TOP_N = 16
WINDOW = 512
PHI_HIDDEN = 2 * HEAD_DIM
Q_BLOCK = 64
N_BRANCH = 3
N_EXPERTS = 16
N_EXPERT_GROUPS = 4
EXPERTS_PER_GROUP = N_EXPERTS // N_EXPERT_GROUPS
TOP_K = 2
D_EXPERT = 512
EPS = 1e-6
NEG_BIG = -1e30

kernel_name = "conv_nsa_yoco_moe_trunk"


def rmsnorm(x, g):
    xf = x.astype(jnp.float32)
    y = xf * lax.rsqrt(jnp.mean(xf * xf, axis=-1, keepdims=True) + EPS)
    return (y * g.astype(jnp.float32)).astype(x.dtype)


def masked_softmax(s, mask):
    s = jnp.where(mask, s.astype(jnp.float32), NEG_BIG)
    p = jax.nn.softmax(s, axis=-1)
    return jnp.where(jnp.any(mask, axis=-1, keepdims=True), p, 0.0)


def conformer_conv(h, w_in, dw, dw_bias, conv_norm, w_out):
    u = h @ w_in
    a, b = jnp.split(u, 2, axis=-1)
    u = a * jax.nn.sigmoid(b)
    u = lax.conv_general_dilated(
        u, dw[:, None, :], window_strides=(1,), padding=((CONV_WIDTH - 1, 0),),
        dimension_numbers=("NWC", "WIO", "NWC"), feature_group_count=CONV_CH) + dw_bias
    u = jax.nn.silu(rmsnorm(u, conv_norm))
    return u @ w_out


def nsa_shared_kv(x, kv_norm, w_kv, cmp_pe, phi_w1, phi_b1, phi_w2, k_norm):
    bsz, s, _ = x.shape
    g, dh = N_KV_GROUPS, HEAD_DIM
    kv = (rmsnorm(x, kv_norm) @ w_kv).reshape(bsz, s, 6, g, dh)
    k_c, v_c, k_s, v_s, k_w, v_w = [kv[:, :, i] for i in range(6)]
    n_cmp = (s - CMP_LEN) // CMP_STRIDE + 1
    idx = jnp.arange(n_cmp)[:, None] * CMP_STRIDE + jnp.arange(CMP_LEN)[None, :]

    def compress(t, pe, w1, b1, w2):
        blk = t[:, idx] + pe[None, None, :, None, :]
        blk = blk.transpose(0, 1, 3, 2, 4).reshape(bsz, n_cmp, g, CMP_LEN * dh)
        return jax.nn.gelu(blk @ w1 + b1) @ w2

    k_cmp = rmsnorm(compress(k_c, cmp_pe[0], phi_w1[0], phi_b1[0], phi_w2[0]), k_norm[0])
    v_cmp = compress(v_c, cmp_pe[1], phi_w1[1], phi_b1[1], phi_w2[1])
    k_s = rmsnorm(k_s, k_norm[1])
    k_w = rmsnorm(k_w, k_norm[2])
    n_slc = s // SLC_LEN
    k_sb = k_s.reshape(bsz, n_slc, SLC_LEN, g, dh).transpose(0, 3, 1, 2, 4)
    v_sb = v_s.reshape(bsz, n_slc, SLC_LEN, g, dh).transpose(0, 3, 1, 2, 4)
    pad = ((0, 0), (WINDOW, 0), (0, 0), (0, 0))
    k_wp = jnp.pad(k_w, pad)
    v_wp = jnp.pad(v_w, pad)
    return k_cmp, v_cmp, k_sb, v_sb, k_wp, v_wp


def nsa_attention(h, w_in, q_norm, w_out, shared):
    k_cmp, v_cmp, k_sb, v_sb, k_wp, v_wp = shared
    bsz, s, _ = h.shape
    g, r, dh = N_KV_GROUPS, HEADS_PER_GROUP, HEAD_DIM
    proj = h @ w_in
    q = rmsnorm(proj[..., :N_HEADS * dh].reshape(bsz, s, g, r, dh), q_norm) * (HEAD_DIM ** -0.5)
    gates = jax.nn.sigmoid(proj[..., N_HEADS * dh:].astype(jnp.float32))
    gates = gates.reshape(bsz, s, g, r, N_BRANCH).astype(h.dtype)
    n_cmp = k_cmp.shape[1]
    n_slc = k_sb.shape[2]
    top_n = min(SLC_TOP_N, n_slc)
    n_blk = s // Q_BLOCK
    cmp_end = jnp.arange(n_cmp) * CMP_STRIDE + CMP_LEN - 1
    ratio, span = SLC_LEN // CMP_STRIDE, CMP_LEN // CMP_STRIDE
    w_np = np.convolve(np.ones(ratio), np.ones(span)).astype(np.float32)
    offs = np.arange(ratio + span - 1) - (span - 1)
    m_idx = ratio * np.arange(n_slc)[:, None] + offs[None, :]
    m_ok = (m_idx >= 0) & (m_idx < n_cmp)
    map_w = jnp.asarray(np.where(m_ok, w_np[None, :], 0.0).astype(np.float32))
    map_idx = jnp.asarray(np.clip(m_idx, 0, n_cmp - 1).astype(np.int32))
    jb = jnp.arange(n_slc)
    gather = jax.vmap(jax.vmap(lambda kb, i: kb[i]))

    def block_fn(args):
        qb, gb, blk = args
        q0 = blk * Q_BLOCK
        t = q0 + jnp.arange(Q_BLOCK)
        s_c = jnp.einsum("bqgrd,bcgd->bgrqc", qb, k_cmp)
        p_c = masked_softmax(s_c, cmp_end[None, :] <= t[:, None])
        o_c = jnp.einsum("bgrqc,bcgd->bqgrd", p_c.astype(v_cmp.dtype), v_cmp)
        imp = (p_c.sum(axis=2)[..., map_idx] * map_w).sum(-1)
        cur = t // SLC_LEN
        forced = (jb[None, :] == 0) | (jb[None, :] == cur[:, None]) | (jb[None, :] == cur[:, None] - 1)
        imp = jnp.where(forced, jnp.inf, imp)
        imp = jnp.where(jb[None, :] > cur[:, None], -jnp.inf, imp)
        _, sel = lax.top_k(imp, top_n)
        k_sel = gather(k_sb, sel).reshape(bsz, g, Q_BLOCK, top_n * SLC_LEN, dh)
        v_sel = gather(v_sb, sel).reshape(bsz, g, Q_BLOCK, top_n * SLC_LEN, dh)
        pos_sel = (sel[..., None] * SLC_LEN + jnp.arange(SLC_LEN)).reshape(bsz, g, Q_BLOCK, top_n * SLC_LEN)
        s_s = jnp.einsum("bqgrd,bgqkd->bgrqk", qb, k_sel)
        p_s = masked_softmax(s_s, (pos_sel <= t[None, None, :, None])[:, :, None])
        o_s = jnp.einsum("bgrqk,bgqkd->bqgrd", p_s.astype(v_sel.dtype), v_sel)
        k_win = lax.dynamic_slice_in_dim(k_wp, q0, WINDOW + Q_BLOCK, axis=1)
        v_win = lax.dynamic_slice_in_dim(v_wp, q0, WINDOW + Q_BLOCK, axis=1)
        kpos = q0 - WINDOW + jnp.arange(WINDOW + Q_BLOCK)
        m_w = (kpos[None, :] <= t[:, None]) & (kpos[None, :] > t[:, None] - WINDOW) & (kpos[None, :] >= 0)
        s_w = jnp.einsum("bqgrd,bkgd->bgrqk", qb, k_win)
        p_w = masked_softmax(s_w, m_w)
        o_w = jnp.einsum("bgrqk,bkgd->bqgrd", p_w.astype(v_win.dtype), v_win)
        return gb[..., 0:1] * o_c + gb[..., 1:2] * o_s + gb[..., 2:3] * o_w

    qs = q.reshape(bsz, n_blk, Q_BLOCK, g, r, dh).swapaxes(0, 1)
    gs = gates.reshape(bsz, n_blk, Q_BLOCK, g, r, N_BRANCH).swapaxes(0, 1)
    o = lax.map(block_fn, (qs, gs, jnp.arange(n_blk)))
    o = o.swapaxes(0, 1).reshape(bsz, s, N_HEADS * dh)
    return o @ w_out


def moe_ffn(h, router_w, router_bias, w1, w3, w2):
    bsz, s, d = h.shape
    hf = h.reshape(-1, d)
    scores = jax.nn.sigmoid((hf @ router_w).astype(jnp.float32))
    biased = scores + router_bias.astype(jnp.float32)
    grp = biased.reshape(-1, N_EXPERT_GROUPS, EXPERTS_PER_GROUP)
    grp_score = lax.top_k(grp, TOP_K)[0].sum(-1)
    g_sel = jnp.argmax(grp_score, axis=-1)
    in_grp = (jnp.arange(N_EXPERTS) // EXPERTS_PER_GROUP)[None, :] == g_sel[:, None]
    _, top_idx = lax.top_k(jnp.where(in_grp, biased, -jnp.inf), TOP_K)
    top_w = jnp.take_along_axis(scores, top_idx, axis=-1)
    top_w = top_w / top_w.sum(-1, keepdims=True)
    gate = (jax.nn.one_hot(top_idx, N_EXPERTS, dtype=jnp.float32) * top_w[..., None]).sum(1).astype(h.dtype)
    out = jnp.zeros_like(hf)
    for e in range(N_EXPERTS):
        y = (jax.nn.silu(hf @ w1[e]) * (hf @ w3[e])) @ w2[e]
        out = out + gate[:, e:e + 1] * y
    return out.reshape(bsz, s, d)


def setup_inputs(seed: int = 0) -> dict:
    key = jax.random.key(seed)
    ks = jax.random.split(key, 32)
    nrm = jax.random.normal
    f32 = jnp.float32
    res = (2.0 * DEPTH) ** -0.5
    qg = N_HEADS * HEAD_DIM + N_BRANCH * N_HEADS
    return {
        "x": nrm(ks[0], (BATCH, SEQ, D_MODEL), f32),
        "a_norm": 1.0 + 0.01 * nrm(ks[1], (N_A_LAYERS, D_MODEL), f32),
        "a_w_in": nrm(ks[2], (N_A_LAYERS, D_MODEL, 2 * CONV_CH), f32) * D_MODEL ** -0.5,
        "a_dw": nrm(ks[3], (N_A_LAYERS, CONV_WIDTH, CONV_CH), f32) * CONV_WIDTH ** -0.5,
        "a_dw_bias": 0.01 * nrm(ks[4], (N_A_LAYERS, CONV_CH), f32),
        "a_conv_norm": 1.0 + 0.01 * nrm(ks[5], (N_A_LAYERS, CONV_CH), f32),
        "a_w_out": nrm(ks[6], (N_A_LAYERS, CONV_CH, D_MODEL), f32) * CONV_CH ** -0.5 * res,
        "kv_norm": 1.0 + 0.01 * nrm(ks[7], (D_MODEL,), f32),
        "w_kv": nrm(ks[8], (D_MODEL, 6 * N_KV_GROUPS * HEAD_DIM), f32) * D_MODEL ** -0.5,
        "cmp_pe": 0.1 * nrm(ks[9], (2, CMP_LEN, HEAD_DIM), f32),
        "phi_w1": nrm(ks[10], (2, CMP_LEN * HEAD_DIM, PHI_HIDDEN), f32) * (CMP_LEN * HEAD_DIM) ** -0.5,
        "phi_b1": 0.01 * nrm(ks[11], (2, PHI_HIDDEN), f32),
        "phi_w2": nrm(ks[12], (2, PHI_HIDDEN, HEAD_DIM), f32) * PHI_HIDDEN ** -0.5,
        "k_norm": 1.0 + 0.01 * nrm(ks[13], (N_BRANCH, HEAD_DIM), f32),
        "b_norm": 1.0 + 0.01 * nrm(ks[14], (N_B_LAYERS, D_MODEL), f32),
        "b_w_in": nrm(ks[15], (N_B_LAYERS, D_MODEL, qg), f32) * D_MODEL ** -0.5,
        "b_q_norm": 1.0 + 0.01 * nrm(ks[16], (N_B_LAYERS, HEAD_DIM), f32),
        "b_w_out": nrm(ks[17], (N_B_LAYERS, N_HEADS * HEAD_DIM, D_MODEL), f32) * (N_HEADS * HEAD_DIM) ** -0.5 * res,
        "ffn_norm": 1.0 + 0.01 * nrm(ks[18], (DEPTH, D_MODEL), f32),
        "router_w": nrm(ks[19], (D_MODEL, N_EXPERTS), f32) * D_MODEL ** -0.5,
        "router_bias": 0.01 * nrm(ks[20], (N_EXPERTS,), f32),
        "moe_w1": nrm(ks[21], (DEPTH, N_EXPERTS, D_MODEL, D_EXPERT), f32) * D_MODEL ** -0.5,
        "moe_w3": nrm(ks[22], (DEPTH, N_EXPERTS, D_MODEL, D_EXPERT), f32) * D_MODEL ** -0.5,
        "moe_w2": nrm(ks[23], (DEPTH, N_EXPERTS, D_EXPERT, D_MODEL), f32) * D_EXPERT ** -0.5 * res,
    }


def reference(x, a_norm, a_w_in, a_dw, a_dw_bias, a_conv_norm, a_w_out, kv_norm, w_kv, cmp_pe,
              phi_w1, phi_b1, phi_w2, k_norm, b_norm, b_w_in, b_q_norm, b_w_out, ffn_norm,
              router_w, router_bias, moe_w1, moe_w3, moe_w2):
    h = x
    shared = None
    for layer in range(DEPTH):
        if layer < N_A_LAYERS:
            i = layer
            h = h + conformer_conv(rmsnorm(h, a_norm[i]), a_w_in[i], a_dw[i], a_dw_bias[i],
                                   a_conv_norm[i], a_w_out[i])
        else:
            i = layer - N_A_LAYERS
            h = h + nsa_attention(rmsnorm(h, b_norm[i]), b_w_in[i], b_q_norm[i], b_w_out[i], shared)
        h = h + moe_ffn(rmsnorm(h, ffn_norm[layer]), router_w, router_bias,
                        moe_w1[layer], moe_w3[layer], moe_w2[layer])
        if layer == N_A_LAYERS - 1:
            shared = nsa_shared_kv(h, kv_norm, w_kv, cmp_pe, phi_w1, phi_b1, phi_w2, k_norm)
    return h
```

```python
import functools

import jax
import jax.numpy as jnp
import numpy as np
from jax import lax
from jax.experimental import pallas as pl
from jax.experimental.pallas import tpu as pltpu

F32 = jnp.float32
BF16 = jnp.bfloat16

EPS = 1e-6
CONV_WIDTH = 31
CONV_PAD = 32
VMEM_LIMIT = 56 * 1024 * 1024


def _rms(x, g):
    ms = jnp.mean(x * x, axis=-1, keepdims=True)
    return x * lax.rsqrt(ms + EPS) * g


def _cparams(sem):
    return pltpu.CompilerParams(dimension_semantics=sem, vmem_limit_bytes=VMEM_LIMIT)


CONV_ROWS = 64
CONV_LANES = 256


def _conv_layer_kernel(h_ref, an_ref, win_ref, dw_ref, dwb_ref, cn_ref, wout_ref, o_ref, ubuf, ybuf):
    ts, c = ybuf.shape
    t = pl.program_id(1)

    @pl.when(t == 0)
    def _():
        ubuf[:, 0:CONV_PAD, :] = jnp.zeros((8, CONV_PAD, c), F32)

    @pl.when(t != 0)
    def _():
        ubuf[:, 0:CONV_PAD, :] = ubuf[:, ts:ts + CONV_PAD, :]

    x = h_ref[...]
    xn = _rms(x, an_ref[...]).astype(BF16)
    ab = jnp.dot(xn, win_ref[...], preferred_element_type=F32)
    u = ab[:, :c] * jax.nn.sigmoid(ab[:, c:])
    for rho in range(8):
        ubuf[rho, CONV_PAD - rho:CONV_PAD - rho + ts, :] = u

    base_off = CONV_PAD - (CONV_WIDTH - 1)

    def chunk(rc, carry):
        r0 = pl.multiple_of(rc * CONV_ROWS, CONV_ROWS)
        for lc in range(c // CONV_LANES):
            lanes = slice(lc * CONV_LANES, (lc + 1) * CONV_LANES)
            acc = jnp.broadcast_to(dwb_ref[:, lanes], (CONV_ROWS, CONV_LANES))
            for rho in range(8):
                taps = [k for k in range(CONV_WIDTH) if (base_off + k) % 8 == rho]
                first = base_off + taps[0] - rho
                nrows = CONV_ROWS + (taps[-1] - taps[0])
                blk = ubuf[rho, pl.ds(r0 + first, nrows), lanes]
                for k in taps:
                    a = k - taps[0]
                    acc = acc + dw_ref[k:k + 1, lanes] * blk[a:a + CONV_ROWS, :]
            ybuf[pl.ds(r0, CONV_ROWS), lanes] = acc
        return carry

    lax.fori_loop(0, ts // CONV_ROWS, chunk, 0)

    y = ybuf[...]
    yn = _rms(y, cn_ref[...])
    act = (yn * jax.nn.sigmoid(yn)).astype(BF16)
    o_ref[...] = x + jnp.dot(act, wout_ref[...], preferred_element_type=F32)


def _conv_layer(h, a_norm, w_in, dw, dw_bias, conv_norm, w_out, *, batch, ts=256):
    n, d = h.shape
    c = w_out.shape[0]
    seq = n // batch
    nts = seq // ts
    dwp = jnp.zeros((CONV_PAD, c), F32).at[:CONV_WIDTH].set(dw)
    row = lambda b, t: (b * nts + t, 0)
    const = lambda b, t: (0, 0)
    return pl.pallas_call(
        _conv_layer_kernel,
        out_shape=jax.ShapeDtypeStruct((n, d), F32),
        grid=(batch, nts),
        in_specs=[
            pl.BlockSpec((ts, d), row),
            pl.BlockSpec((1, d), const),
            pl.BlockSpec((d, 2 * c), const),
            pl.BlockSpec((CONV_PAD, c), const),
            pl.BlockSpec((1, c), const),
            pl.BlockSpec((1, c), const),
            pl.BlockSpec((c, d), const),
        ],
        out_specs=pl.BlockSpec((ts, d), row),
        scratch_shapes=[pltpu.VMEM((8, ts + CONV_PAD, c), F32), pltpu.VMEM((ts, c), F32)],
        compiler_params=_cparams(("arbitrary", "arbitrary")),
        name="conv_layer",
    )(h, a_norm.reshape(1, d), w_in.astype(BF16), dwp, dw_bias.reshape(1, c),
      conv_norm.reshape(1, c), w_out.astype(BF16))


N_EXPERTS = 16
N_GROUPS = 4
GROUP_SIZE = N_EXPERTS // N_GROUPS
PAIRS = ((0, 1), (0, 2), (0, 3), (1, 2), (1, 3), (2, 3))
N_BUCKETS = N_GROUPS * len(PAIRS)
BUCKET_ROWS = 32
LANES = 128
NEG_INF = float("-inf")


def _split_bf16(x):
    hi = x.astype(BF16)
    lo = (x - hi.astype(F32)).astype(BF16)
    return hi, lo


def _router_logits(xn, wr_ref):
    hi, lo = _split_bf16(xn)
    w = wr_ref[...]
    both = jnp.dot(hi, w, preferred_element_type=F32)
    low = jnp.dot(lo, w[:, :LANES], preferred_element_type=F32)
    return both[:, :LANES] + both[:, LANES:] + low, hi


def _route_kernel(bias_ref, h_ref, fn_ref, wr_ref, tri_ref, bucket_ref, rank_ref, counts_ref, carry):
    tr = h_ref.shape[0]

    @pl.when(pl.program_id(0) == 0)
    def _():
        carry[...] = jnp.zeros_like(carry)

    xn = _rms(h_ref[...], fn_ref[...])
    logits, _ = _router_logits(xn, wr_ref)
    lt = logits.T
    score = jax.nn.sigmoid(lt[0:N_EXPERTS, :])
    biased = [score[e:e + 1, :] + bias_ref[e] for e in range(N_EXPERTS)]

    best = None
    for g in range(N_GROUPS):
        v = biased[GROUP_SIZE * g:GROUP_SIZE * (g + 1)]
        gs = None
        for a, b in PAIRS:
            gs = v[a] + v[b] if gs is None else jnp.maximum(gs, v[a] + v[b])
        if best is None:
            best, gsel = gs, jnp.zeros_like(gs, dtype=jnp.int32)
        else:
            better = gs > best
            gsel = jnp.where(better, g, gsel)
            best = jnp.where(better, gs, best)

    vb = []
    for i in range(GROUP_SIZE):
        val = biased[i]
        for g in range(1, N_GROUPS):
            val = jnp.where(gsel == g, biased[GROUP_SIZE * g + i], val)
        vb.append(val)

    def first_argmax(vals):
        m, idx = vals[0], jnp.zeros_like(gsel)
        for i in range(1, GROUP_SIZE):
            better = vals[i] > m
            idx = jnp.where(better, i, idx)
            m = jnp.where(better, vals[i], m)
        return idx

    i1 = first_argmax(vb)
    i2 = first_argmax([jnp.where(i1 == i, NEG_INF, vb[i]) for i in range(GROUP_SIZE)])
    lo = jnp.minimum(i1, i2)
    hi = jnp.maximum(i1, i2)
    pair = jnp.where(lo == 0, 0, jnp.where(lo == 1, 3, 5)) + hi - lo - 1
    bucket = gsel * len(PAIRS) + pair

    onehot = lax.broadcasted_iota(jnp.int32, (BUCKET_ROWS, tr), 0) == bucket
    before = jnp.dot(onehot.astype(BF16), tri_ref[...], preferred_element_type=F32)
    onef = onehot.astype(F32)
    rank = jnp.sum(onef * (before + carry[:, 0:1]), axis=0, keepdims=True)
    carry[...] = carry[...] + jnp.sum(onef, axis=1, keepdims=True)
    bucket_ref[0] = bucket
    rank_ref[0] = rank.astype(jnp.int32)
    counts_ref[...] = carry[...]


def _moe_route(h, fn, wr, bias, *, tr=512):
    n, d = h.shape
    nt = n // tr
    tri = jnp.triu(jnp.ones((tr, tr), F32), 1).astype(BF16)
    const = lambda i, *_: (0, 0)
    bucket, rank, counts = pl.pallas_call(
        _route_kernel,
        out_shape=(jax.ShapeDtypeStruct((nt, 1, tr), jnp.int32),
                   jax.ShapeDtypeStruct((nt, 1, tr), jnp.int32),
                   jax.ShapeDtypeStruct((BUCKET_ROWS, LANES), F32)),
        grid_spec=pltpu.PrefetchScalarGridSpec(
            num_scalar_prefetch=1,
            grid=(nt,),
            in_specs=[
                pl.BlockSpec((tr, d), lambda i, *_: (i, 0)),
                pl.BlockSpec((1, d), const),
                pl.BlockSpec((d, 2 * LANES), const),
                pl.BlockSpec((tr, tr), const),
            ],
            out_specs=[
                pl.BlockSpec((1, 1, tr), lambda i, *_: (i, 0, 0)),
                pl.BlockSpec((1, 1, tr), lambda i, *_: (i, 0, 0)),
                pl.BlockSpec((BUCKET_ROWS, LANES), const),
            ],
            scratch_shapes=[pltpu.VMEM((BUCKET_ROWS, LANES), F32)],
        ),
        compiler_params=_cparams(("arbitrary",)),
        name="moe_route",
    )(bias, h, fn, wr, tri)
    return bucket, rank, counts


def _permute_kernel(offs_ref, bucket_ref, rank_ref, src_ref, *rest, ch, scatter):
    dst_ref, sem = rest[-2], rest[-1]
    base = pl.program_id(0) * ch

    def copy(j):
        sorted_row = offs_ref[bucket_ref[0, 0, j]] + rank_ref[0, 0, j]
        src_row, dst_row = (base + j, sorted_row) if scatter else (sorted_row, base + j)
        return pltpu.make_async_copy(src_ref.at[pl.ds(src_row, 1)], dst_ref.at[pl.ds(dst_row, 1)], sem)

    def start(j, c):
        copy(j).start()
        return c

    lax.fori_loop(0, ch, start, 0, unroll=8)
    pltpu.make_async_copy(src_ref.at[pl.ds(0, ch)], dst_ref.at[pl.ds(0, ch)], sem).wait()


def _permute_rows(src, offs, bucket, rank, *, n_out, scatter, ch=1024):
    n = bucket.size
    d = src.shape[1]
    ch = min(ch, n)
    bucket = bucket.reshape(n // ch, 1, ch)
    rank = rank.reshape(n // ch, 1, ch)
    smem_blk = pl.BlockSpec((1, 1, ch), lambda i, *_: (i, 0, 0), memory_space=pltpu.SMEM)
    any_spec = pl.BlockSpec(memory_space=pl.ANY)
    args = [offs, bucket, rank, src]
    in_specs = [smem_blk, smem_blk, any_spec]
    aliases = {}
    if scatter:
        args.append(jnp.zeros((n_out, d), src.dtype))
        in_specs.append(any_spec)
        aliases = {4: 0}
    return pl.pallas_call(
        functools.partial(_permute_kernel, ch=ch, scatter=scatter),
        out_shape=jax.ShapeDtypeStruct((n_out, d), src.dtype),
        grid_spec=pltpu.PrefetchScalarGridSpec(
            num_scalar_prefetch=1,
            grid=(n // ch,),
            in_specs=in_specs,
            out_specs=any_spec,
            scratch_shapes=[pltpu.SemaphoreType.DMA(())],
        ),
        input_output_aliases=aliases,
        compiler_params=_cparams(("arbitrary",)),
        name="moe_scatter" if scatter else "moe_gather",
    )(*args)


def _ffn_kernel(ea_ref, eb_ref, nv_ref, hs_ref, fn_ref, wr_ref, w1a, w3a, w2a, w1b, w3b, w2b, o_ref):
    i = pl.program_id(0)

    @pl.when(nv_ref[i] == 0)
    def _():
        o_ref[...] = hs_ref[...]

    @pl.when(nv_ref[i] > 0)
    def _():
        x = hs_ref[...]
        xn = _rms(x, fn_ref[...])
        logits, xb = _router_logits(xn, wr_ref)
        score = jax.nn.sigmoid(logits)
        lane = lax.broadcasted_iota(jnp.int32, score.shape, 1)
        sa = jnp.sum(jnp.where(lane == ea_ref[i], score, 0.0), axis=-1, keepdims=True)
        sb = jnp.sum(jnp.where(lane == eb_ref[i], score, 0.0), axis=-1, keepdims=True)
        tot = sa + sb

        def expert(w1, w3, w2, gate):
            h1 = jnp.dot(xb, w1[...], preferred_element_type=F32)
            h3 = jnp.dot(xb, w3[...], preferred_element_type=F32)
            act = (h1 * jax.nn.sigmoid(h1)) * h3 * gate
            return jnp.dot(act.astype(BF16), w2[...], preferred_element_type=F32)

        o_ref[...] = x + expert(w1a, w3a, w2a, sa / tot) + expert(w1b, w3b, w2b, sb / tot)


def _moe_ffn(hs, fn, wr, w1, w3, w2, tile_ea, tile_eb, tile_nv, *, tm):
    ns, d = hs.shape
    de = w1.shape[2]
    ntiles = ns // tm
    const = lambda i, *_: (0, 0)
    wa = lambda i, ea, eb, nv: (ea[i], 0, 0)
    wb = lambda i, ea, eb, nv: (eb[i], 0, 0)
    return pl.pallas_call(
        _ffn_kernel,
        out_shape=jax.ShapeDtypeStruct((ns, d), F32),
        grid_spec=pltpu.PrefetchScalarGridSpec(
            num_scalar_prefetch=3,
            grid=(ntiles,),
            in_specs=[
                pl.BlockSpec((tm, d), lambda i, *_: (i, 0)),
                pl.BlockSpec((1, d), const),
                pl.BlockSpec((d, 2 * LANES), const),
                pl.BlockSpec((None, d, de), wa),
                pl.BlockSpec((None, d, de), wa),
                pl.BlockSpec((None, de, d), wa),
                pl.BlockSpec((None, d, de), wb),
                pl.BlockSpec((None, d, de), wb),
                pl.BlockSpec((None, de, d), wb),
            ],
            out_specs=pl.BlockSpec((tm, d), lambda i, *_: (i, 0)),
        ),
        compiler_params=_cparams(("arbitrary",)),
        name="moe_ffn",
    )(tile_ea, tile_eb, tile_nv, hs, fn, wr, w1, w3, w2, w1, w3, w2)


def _router_weights(router_w):
    d = router_w.shape[0]
    hi, lo = _split_bf16(router_w)
    wr = jnp.zeros((d, 2 * LANES), BF16)
    return wr.at[:, :N_EXPERTS].set(hi).at[:, LANES:LANES + N_EXPERTS].set(lo)


def _moe_layer(h, fn, wr, bias, w1, w3, w2, *, tm=512):
    n, d = h.shape
    fn = fn.reshape(1, d)
    bucket, rank, counts = _moe_route(h, fn, wr, bias)
    counts = counts[:N_BUCKETS, 0].astype(jnp.int32)
    ntile_b = (counts + tm - 1) // tm
    tile_end = jnp.cumsum(ntile_b)
    tile_start = tile_end - ntile_b
    offs = jnp.zeros((BUCKET_ROWS,), jnp.int32).at[:N_BUCKETS].set(tile_start * tm)
    ntiles = n // tm + N_BUCKETS
    j = jnp.arange(ntiles, dtype=jnp.int32)
    used = j < tile_end[-1]
    tb = jnp.minimum(jnp.searchsorted(tile_end, j, side="right").astype(jnp.int32), N_BUCKETS - 1)
    last_b = jnp.max(jnp.where(ntile_b > 0, jnp.arange(N_BUCKETS, dtype=jnp.int32), 0))
    tb = jnp.where(used, tb, last_b)
    nv = jnp.where(used, jnp.clip(counts[tb] - (j - tile_start[tb]) * tm, 0, tm), 0).astype(jnp.int32)
    pair_lo = jnp.asarray([p[0] for p in PAIRS], jnp.int32)
    pair_hi = jnp.asarray([p[1] for p in PAIRS], jnp.int32)
    ea = GROUP_SIZE * (tb // len(PAIRS)) + pair_lo[tb % len(PAIRS)]
    eb = GROUP_SIZE * (tb // len(PAIRS)) + pair_hi[tb % len(PAIRS)]
    hs = _permute_rows(h, offs, bucket, rank, n_out=ntiles * tm, scatter=True)
    ys = _moe_ffn(hs, fn, wr, w1, w3, w2, ea, eb, nv, tm=tm)
    return _permute_rows(ys, offs, bucket, rank, n_out=n, scatter=False)


N_HEADS = 16
N_KV_GROUPS = 4
HEADS_PER_GROUP = N_HEADS // N_KV_GROUPS
HEAD_DIM = 64
CMP_LEN = 32
CMP_STRIDE = 16
SLC_LEN = 64
SLC_SHIFT = 6
SLC_TOP_N = 16
WINDOW = 512
N_BRANCH = 3
MASKED = -1e30
NT_DIMS = (((1,), (1,)), ((), ()))


def _segment_matrix():
    seg = np.kron(np.eye(LANES // HEAD_DIM), np.full((HEAD_DIM, HEAD_DIM), 1.0 / HEAD_DIM))
    return jnp.asarray(seg, BF16)


def _placement_matrix():
    p = np.zeros((LANES, 4 * LANES), np.float32)
    for half in range(2):
        for l in range(HEAD_DIM):
            p[half * HEAD_DIM + l, half * 2 * LANES + l] = 1.0
            p[half * HEAD_DIM + l, half * 2 * LANES + 3 * HEAD_DIM + l] = 1.0
    return jnp.asarray(p, BF16)


def _seg_rms(x, seg, gain):
    hi, lo = _split_bf16(x * x)
    ms = jnp.dot(hi, seg, preferred_element_type=F32) + jnp.dot(lo, seg, preferred_element_type=F32)
    return x * lax.rsqrt(ms + EPS) * gain


def _kv_proj_kernel(x_ref, kvn_ref, wkv_ref, seg_ref, pslab_ref, kgain_ref,
                    kcvc_ref, ks_ref, vs_ref, kw_ref, vw_ref):
    xn = _rms(x_ref[...], kvn_ref[...]).astype(BF16)
    kv = jnp.dot(xn, wkv_ref[...], preferred_element_type=F32)
    width = N_KV_GROUPS * HEAD_DIM
    for s in range(4):
        kcvc_ref[s] = kv[:, LANES * s:LANES * (s + 1)]
    seg = seg_ref[...]
    pslab = pslab_ref[...]

    def place(i, out_ref, gain_row):
        for s in range(width // LANES):
            slab = kv[:, width * i + LANES * s:width * i + LANES * (s + 1)]
            if gain_row is not None:
                slab = _seg_rms(slab, seg, kgain_ref[gain_row:gain_row + 1, :])
            out_ref[:, 4 * LANES * s:4 * LANES * (s + 1)] = jnp.dot(
                slab.astype(BF16), pslab, preferred_element_type=F32).astype(BF16)

    place(2, ks_ref, 1)
    place(3, vs_ref, None)
    place(4, kw_ref, 2)
    place(5, vw_ref, None)


def _kv_proj(x, kv_norm, w_kv, k_norm, *, t=512):
    n, d = x.shape
    width = N_KV_GROUPS * HEAD_DIM
    kgain = jnp.zeros((8, LANES), F32).at[:N_BRANCH].set(jnp.tile(k_norm, (1, LANES // HEAD_DIM)))
    const = lambda i: (0, 0)
    wide = jax.ShapeDtypeStruct((n, 4 * width), BF16)
    wide_spec = pl.BlockSpec((t, 4 * width), lambda i: (i, 0))
    return pl.pallas_call(
        _kv_proj_kernel,
        out_shape=(jax.ShapeDtypeStruct((4, n, LANES), F32), wide, wide, wide, wide),
        grid=(n // t,),
        in_specs=[
            pl.BlockSpec((t, d), lambda i: (i, 0)),
            pl.BlockSpec((1, d), const),
            pl.BlockSpec((d, 6 * width), const),
            pl.BlockSpec((LANES, LANES), const),
            pl.BlockSpec((LANES, 4 * LANES), const),
            pl.BlockSpec((8, LANES), const),
        ],
        out_specs=(pl.BlockSpec((4, t, LANES), lambda i: (0, i, 0)), wide_spec, wide_spec, wide_spec, wide_spec),
        compiler_params=_cparams(("arbitrary",)),
        name="nsa_kv_proj",
    )(x, kv_norm.reshape(1, d), w_kv.astype(BF16), _segment_matrix(), _placement_matrix(), kgain)


def _compress_kernel(x_ref, pe_ref, w1_ref, b1_ref, w2_ref, seg_ref, pslab_ref, gain_ref, o_ref, bm_scr,
                     *, normalize):
    seq = x_ref.shape[1]
    nc = seq // CMP_STRIDE
    halves = CMP_LEN // CMP_STRIDE
    assert halves == 2
    hidden = w1_ref.shape[2]
    for slab in range(2):
        first = jnp.zeros((nc, hidden), F32)
        second = jnp.zeros((nc, hidden), F32)
        for j in range(CMP_STRIDE):
            xj = x_ref[slab, pl.ds(j, nc, stride=CMP_STRIDE), :]
            first += jnp.dot((xj + pe_ref[j:j + 1, :]).astype(BF16), w1_ref[j], preferred_element_type=F32)
            second += jnp.dot((xj + pe_ref[CMP_STRIDE + j:CMP_STRIDE + j + 1, :]).astype(BF16),
                              w1_ref[CMP_STRIDE + j], preferred_element_type=F32)
        bm_scr[0:nc, :] = second
        bm_scr[nc:nc + 8, :] = jnp.zeros((8, hidden), F32)
        z = first + bm_scr[1:nc + 1, :] + b1_ref[...]
        hid = jax.nn.gelu(z)
        out = jnp.dot(hid.astype(BF16), w2_ref[...], preferred_element_type=F32)
        if normalize:
            out = _seg_rms(out, seg_ref[...], gain_ref[0:1, :])
        out = jnp.where(lax.broadcasted_iota(jnp.int32, out.shape, 0) < nc - 1, out, 0.0)
        o_ref[:, 4 * LANES * slab:4 * LANES * (slab + 1)] = jnp.dot(
            out.astype(BF16), pslab_ref[...], preferred_element_type=F32).astype(BF16)


def _compress(kcvc, which, pe, w1, b1, w2, gain, *, batch, normalize):
    n = kcvc.shape[1]
    seq = n // batch
    nc = seq // CMP_STRIDE
    hidden = w1.shape[1]
    eye2 = jnp.eye(2, dtype=F32)
    w1j = w1.reshape(CMP_LEN, HEAD_DIM, hidden)
    w1bd = jax.vmap(lambda m: jnp.kron(eye2, m))(w1j).astype(BF16)
    w2bd = jnp.kron(eye2, w2).astype(BF16)
    pe2 = jnp.tile(pe, (1, 2))
    b12 = jnp.tile(b1.reshape(1, hidden), (1, 2))
    gain2 = jnp.zeros((8, LANES), F32).at[0].set(jnp.tile(gain, 2))
    const2 = lambda b: (0, 0)
    return pl.pallas_call(
        functools.partial(_compress_kernel, normalize=normalize),
        out_shape=jax.ShapeDtypeStruct((batch, nc, 8 * LANES), BF16),
        grid=(batch,),
        in_specs=[
            pl.BlockSpec((2, seq, LANES), lambda b: (which, b, 0)),
            pl.BlockSpec((CMP_LEN, LANES), const2),
            pl.BlockSpec((CMP_LEN, LANES, 2 * hidden), lambda b: (0, 0, 0)),
            pl.BlockSpec((1, 2 * hidden), const2),
            pl.BlockSpec((2 * hidden, LANES), const2),
            pl.BlockSpec((LANES, LANES), const2),
            pl.BlockSpec((LANES, 4 * LANES), const2),
            pl.BlockSpec((8, LANES), const2),
        ],
        out_specs=pl.BlockSpec((None, nc, 8 * LANES), lambda b: (b, 0, 0)),
        scratch_shapes=[pltpu.VMEM((nc + 8, 2 * hidden), F32)],
        compiler_params=_cparams(("arbitrary",)),
        name="nsa_compress",
    )(kcvc, pe2, w1bd, b12, w2bd, _segment_matrix(), _placement_matrix(), gain2)


def _nsa_proj_kernel(h_ref, bn_ref, win_ref, seg_ref, qgain_ref, q_ref, g_ref):
    xn = _rms(h_ref[...], bn_ref[...]).astype(BF16)
    proj = jnp.dot(xn, win_ref[...], preferred_element_type=F32)
    nq = q_ref.shape[1]
    seg = seg_ref[...]
    for s in range(nq // LANES):
        slab = proj[:, LANES * s:LANES * (s + 1)]
        q_ref[:, LANES * s:LANES * (s + 1)] = _seg_rms(slab, seg, qgain_ref[...]).astype(BF16)
    g_ref[...] = jax.nn.sigmoid(proj[:, nq:nq + LANES])


def _nsa_proj(h, b_norm, w_in, q_norm, *, t=512):
    n, d = h.shape
    nq = N_HEADS * HEAD_DIM
    win = jnp.zeros((d, nq + LANES), BF16).at[:, :w_in.shape[1]].set(w_in.astype(BF16))
    qgain = jnp.tile(q_norm * HEAD_DIM ** -0.5, LANES // HEAD_DIM).reshape(1, LANES)
    const = lambda i: (0, 0)
    return pl.pallas_call(
        _nsa_proj_kernel,
        out_shape=(jax.ShapeDtypeStruct((n, nq), BF16), jax.ShapeDtypeStruct((n, LANES), F32)),
        grid=(n // t,),
        in_specs=[
            pl.BlockSpec((t, d), lambda i: (i, 0)),
            pl.BlockSpec((1, d), const),
            pl.BlockSpec((d, nq + LANES), const),
            pl.BlockSpec((LANES, LANES), const),
            pl.BlockSpec((1, LANES), const),
        ],
        out_specs=(pl.BlockSpec((t, nq), lambda i: (i, 0)), pl.BlockSpec((t, LANES), lambda i: (i, 0))),
        compiler_params=_cparams(("arbitrary",)),
        name="nsa_proj",
    )(h, b_norm.reshape(1, d), win, _segment_matrix(), qgain)


def _masked_softmax(s, valid):
    sm = jnp.where(valid, s, MASKED)
    m = jnp.max(sm, axis=-1, keepdims=True)
    e = jnp.where(valid, jnp.exp(sm - m), 0.0)
    tot = jnp.sum(e, axis=-1, keepdims=True)
    return e / jnp.where(tot > 0.0, tot, 1.0)


def _nsa_attn_kernel(q_ref, g_ref, kc_ref, vc_ref, ks_ref, vs_ref, kw_ref, vw_ref, mapm_ref, gsel_ref,
                     o_ref, me_scr, mo_scr, le_scr, lo_scr, acc_scr, *, tq, tk):
    q0 = pl.program_id(2) * tq
    seq = ks_ref.shape[0]
    qq = jnp.concatenate([q_ref[:, 0:LANES], q_ref[:, LANES:2 * LANES]], axis=0)
    t = q0 + lax.broadcasted_iota(jnp.int32, (tq, 1), 0)
    t2 = jnp.concatenate([t, t], axis=0)
    lane_lo = lax.broadcasted_iota(jnp.int32, (2 * tq, LANES), 1) < HEAD_DIM

    def scores(k_blk):
        return (lax.dot_general(qq, k_blk[:, :LANES], NT_DIMS, preferred_element_type=F32),
                lax.dot_general(qq, k_blk[:, LANES:], NT_DIMS, preferred_element_type=F32))

    def weighted(p_even, p_odd, v_blk):
        return (jnp.dot(p_even.astype(BF16), v_blk[:, :LANES], preferred_element_type=F32)
                + jnp.dot(p_odd.astype(BF16), v_blk[:, LANES:], preferred_element_type=F32))

    kc = kc_ref[...]
    nc = kc.shape[0]
    s_e, s_o = scores(kc)
    cmp_end = lax.broadcasted_iota(jnp.int32, (1, nc), 1) * CMP_STRIDE + (CMP_LEN - 1)
    valid_c = cmp_end <= t2
    p_e = _masked_softmax(s_e, valid_c)
    p_o = _masked_softmax(s_o, valid_c)
    o_cmp = weighted(p_e, p_o, vc_ref[...])
    p_sum = (p_e[:tq] + p_o[:tq]) + (p_e[tq:] + p_o[tq:])
    hi, lo = _split_bf16(p_sum)
    mapm = mapm_ref[...]
    imp = jnp.dot(hi, mapm, preferred_element_type=F32) + jnp.dot(lo, mapm, preferred_element_type=F32)

    blk = lax.broadcasted_iota(jnp.int32, (tq, LANES), 1)
    cur = jnp.right_shift(t, SLC_SHIFT)
    forced = (blk == 0) | (blk == cur) | (blk == cur - 1)
    work = jnp.where(forced, jnp.inf, imp)
    work = jnp.where(blk > cur, NEG_INF, work)
    sel = jnp.zeros((tq, LANES), F32)
    for _ in range(SLC_TOP_N):
        m = jnp.max(work, axis=-1, keepdims=True)
        idx = jnp.min(jnp.where(work == m, blk, LANES), axis=-1, keepdims=True)
        pick = blk == idx
        sel = jnp.where(pick, 1.0, sel)
        work = jnp.where(pick, NEG_INF, work)
    sel2 = jnp.concatenate([sel, sel], axis=0).astype(BF16)

    me_scr[...] = jnp.full(me_scr.shape, MASKED, F32)
    mo_scr[...] = jnp.full(mo_scr.shape, MASKED, F32)
    le_scr[...] = jnp.zeros(le_scr.shape, F32)
    lo_scr[...] = jnp.zeros(lo_scr.shape, F32)
    acc_scr[...] = jnp.zeros(acc_scr.shape, F32)

    def key_tile(kt, carry):
        k0 = pl.multiple_of(kt * tk, tk)
        kpos = k0 + lax.broadcasted_iota(jnp.int32, (1, tk), 1)
        expand = (lax.broadcasted_iota(jnp.int32, (LANES, tk), 0)
                  == jnp.right_shift(k0 + lax.broadcasted_iota(jnp.int32, (LANES, tk), 1), SLC_SHIFT))
        chosen = jnp.dot(sel2, jnp.where(expand, 1.0, 0.0).astype(BF16), preferred_element_type=F32)
        mask = (chosen > 0.5) & (kpos <= t2)
        s_e, s_o = scores(ks_ref[pl.ds(k0, tk), :])

        def update(s, m_scr, l_scr):
            m_old = m_scr[...]
            m_new = jnp.maximum(m_old, jnp.max(jnp.where(mask, s, MASKED), axis=-1, keepdims=True))
            p = jnp.where(mask, jnp.exp(s - m_new[:, 0:1]), 0.0)
            alpha = jnp.exp(m_old - m_new)
            l_scr[...] = alpha * l_scr[...] + jnp.sum(p, axis=-1, keepdims=True)
            m_scr[...] = m_new
            return p, alpha

        p_e, a_e = update(s_e, me_scr, le_scr)
        p_o, a_o = update(s_o, mo_scr, lo_scr)
        acc_scr[...] = (acc_scr[...] * jnp.where(lane_lo, a_e, a_o)
                        + weighted(p_e, p_o, vs_ref[pl.ds(k0, tk), :]))
        return carry

    lax.fori_loop(0, (q0 + tq + tk - 1) // tk, key_tile, 0)
    o_sel = acc_scr[...] / jnp.where(lane_lo, le_scr[...], lo_scr[...])

    span = WINDOW + tq
    w0 = pl.multiple_of(jnp.clip(q0 - WINDOW, 0, seq - span), tq)
    wpos = w0 + lax.broadcasted_iota(jnp.int32, (1, span), 1)
    valid_w = (wpos <= t2) & (wpos > t2 - WINDOW)
    s_e, s_o = scores(kw_ref[pl.ds(w0, span), :])
    o_win = weighted(_masked_softmax(s_e, valid_w), _masked_softmax(s_o, valid_w), vw_ref[pl.ds(w0, span), :])

    gh, gl = _split_bf16(g_ref[...])
    gsel = gsel_ref[...]
    gx = jnp.dot(gh, gsel, preferred_element_type=F32) + jnp.dot(gl, gsel, preferred_element_type=F32)
    out = jnp.zeros((2 * tq, LANES), F32)
    for br, o_br in enumerate((o_cmp, o_sel, o_win)):
        gate = jnp.concatenate([gx[:, LANES * br:LANES * (br + 1)],
                                gx[:, LANES * (N_BRANCH + br):LANES * (N_BRANCH + br + 1)]], axis=0)
        out = out + gate * o_br
    o_ref[:, 0:LANES] = out[:tq].astype(o_ref.dtype)
    o_ref[:, LANES:2 * LANES] = out[tq:].astype(o_ref.dtype)


def _importance_map(nc, n_cmp):
    ratio, span = SLC_LEN // CMP_STRIDE, CMP_LEN // CMP_STRIDE
    w = np.convolve(np.ones(ratio), np.ones(span))
    m = np.zeros((nc, LANES), np.float32)
    for j in range(LANES):
        for o, wt in enumerate(w):
            c = ratio * j + o - (span - 1)
            if 0 <= c < n_cmp:
                m[c, j] = wt
    return jnp.asarray(m, BF16)


def _gate_select():
    m = np.zeros((N_KV_GROUPS, LANES, 2 * N_BRANCH * LANES), np.float32)
    for g in range(N_KV_GROUPS):
        for pair in range(2):
            for br in range(N_BRANCH):
                for lane in range(LANES):
                    r = 2 * pair + lane // HEAD_DIM
                    m[g, (g * HEADS_PER_GROUP + r) * N_BRANCH + br, (pair * N_BRANCH + br) * LANES + lane] = 1.0
    return jnp.asarray(m, BF16)


def _nsa_attention(q, gates, shared, *, batch, tq=256, tk=512):
    kcmp, vcmp, ks, vs, kw, vw = shared
    n = q.shape[0]
    seq = n // batch
    nq = seq // tq
    nc = kcmp.shape[1]
    width = 2 * LANES
    ks, vs, kw, vw = (a.reshape(batch, seq, N_KV_GROUPS * width) for a in (ks, vs, kw, vw))
    qrow = lambda b, g, i: (b * nq + i, g)
    per_bg = lambda b, g, i: (b, 0, g)
    stat = pltpu.VMEM((2 * tq, LANES), F32)
    return pl.pallas_call(
        functools.partial(_nsa_attn_kernel, tq=tq, tk=tk),
        out_shape=jax.ShapeDtypeStruct((n, N_HEADS * HEAD_DIM), BF16),
        grid=(batch, N_KV_GROUPS, nq),
        in_specs=[
            pl.BlockSpec((tq, width), qrow),
            pl.BlockSpec((tq, LANES), lambda b, g, i: (b * nq + i, 0)),
            pl.BlockSpec((None, nc, width), per_bg),
            pl.BlockSpec((None, nc, width), per_bg),
            pl.BlockSpec((None, seq, width), per_bg),
            pl.BlockSpec((None, seq, width), per_bg),
            pl.BlockSpec((None, seq, width), per_bg),
            pl.BlockSpec((None, seq, width), per_bg),
            pl.BlockSpec((nc, LANES), lambda b, g, i: (0, 0)),
            pl.BlockSpec((None, LANES, 2 * N_BRANCH * LANES), lambda b, g, i: (g, 0, 0)),
        ],
        out_specs=pl.BlockSpec((tq, width), qrow),
        scratch_shapes=[stat, stat, stat, stat, stat],
        compiler_params=_cparams(("arbitrary", "arbitrary", "arbitrary")),
        name="nsa_attention",
    )(q, gates, kcmp, vcmp, ks, vs, kw, vw, _importance_map(nc, (seq - CMP_LEN) // CMP_STRIDE + 1), _gate_select())


def _out_proj_kernel(h_ref, o_ref, w_ref, out_ref):
    out_ref[...] = h_ref[...] + jnp.dot(o_ref[...], w_ref[...], preferred_element_type=F32)


def _out_proj(h, o, w_out, *, t=512):
    n, d = h.shape
    k = o.shape[1]
    return pl.pallas_call(
        _out_proj_kernel,
        out_shape=jax.ShapeDtypeStruct((n, d), F32),
        grid=(n // t,),
        in_specs=[pl.BlockSpec((t, d), lambda i: (i, 0)), pl.BlockSpec((t, k), lambda i: (i, 0)),
                  pl.BlockSpec((k, d), lambda i: (0, 0))],
        out_specs=pl.BlockSpec((t, d), lambda i: (i, 0)),
        compiler_params=_cparams(("arbitrary",)),
        name="nsa_out_proj",
    )(h, o, w_out.astype(BF16))


def _nsa_shared_kv(x, kv_norm, w_kv, cmp_pe, phi_w1, phi_b1, phi_w2, k_norm, *, batch):
    kcvc, ks, vs, kw, vw = _kv_proj(x, kv_norm, w_kv, k_norm)
    kcmp = _compress(kcvc, 0, cmp_pe[0], phi_w1[0], phi_b1[0], phi_w2[0], k_norm[0], batch=batch, normalize=True)
    vcmp = _compress(kcvc, 1, cmp_pe[1], phi_w1[1], phi_b1[1], phi_w2[1], k_norm[0], batch=batch, normalize=False)
    return kcmp, vcmp, ks, vs, kw, vw


def _nsa_layer(h, b_norm, w_in, q_norm, w_out, shared, *, batch):
    q, gates = _nsa_proj(h, b_norm, w_in, q_norm)
    o = _nsa_attention(q, gates, shared, batch=batch)
    return _out_proj(h, o, w_out)


def kernel(x, a_norm, a_w_in, a_dw, a_dw_bias, a_conv_norm, a_w_out, kv_norm, w_kv, cmp_pe, phi_w1, phi_b1, phi_w2, k_norm, b_norm, b_w_in, b_q_norm, b_w_out, ffn_norm, router_w, router_bias, moe_w1, moe_w3, moe_w2):
    bsz, s, d = x.shape
    h = x.reshape(bsz * s, d)
    wr = _router_weights(router_w)
    n_a = a_norm.shape[0]
    depth = ffn_norm.shape[0]
    shared = None
    for layer in range(depth):
        if layer < n_a:
            i = layer
            h = _conv_layer(h, a_norm[i], a_w_in[i], a_dw[i], a_dw_bias[i], a_conv_norm[i], a_w_out[i], batch=bsz)
        else:
            i = layer - n_a
            h = _nsa_layer(h, b_norm[i], b_w_in[i], b_q_norm[i], b_w_out[i], shared, batch=bsz)
        h = _moe_layer(h, ffn_norm[layer], wr, router_bias, moe_w1[layer].astype(BF16),
                       moe_w3[layer].astype(BF16), moe_w2[layer].astype(BF16))
        if layer == n_a - 1:
            shared = _nsa_shared_kv(h, kv_norm, w_kv, cmp_pe, phi_w1, phi_b1, phi_w2, k_norm, batch=bsz)
    return h.reshape(bsz, s, d)
```

```python
import functools

import jax
import jax.numpy as jnp
import numpy as np
from jax import lax
from jax.experimental import pallas as pl
from jax.experimental.pallas import tpu as pltpu

F32 = jnp.float32
BF16 = jnp.bfloat16

EPS = 1e-6
CONV_WIDTH = 31
CONV_PAD = 32
VMEM_LIMIT = 56 * 1024 * 1024


def _rms(x, g):
    ms = jnp.mean(x * x, axis=-1, keepdims=True)
    return x * lax.rsqrt(ms + EPS) * g


def _cparams(sem):
    return pltpu.CompilerParams(dimension_semantics=sem, vmem_limit_bytes=VMEM_LIMIT)


CONV_ROWS = 64
CONV_LANES = 256


def _conv_layer_kernel(h_ref, an_ref, win_ref, dw_ref, dwb_ref, cn_ref, wout_ref, o_ref, ubuf, ybuf):
    ts, c = ybuf.shape
    t = pl.program_id(1)

    @pl.when(t == 0)
    def _():
        ubuf[:, 0:CONV_PAD, :] = jnp.zeros((8, CONV_PAD, c), F32)

    @pl.when(t != 0)
    def _():
        ubuf[:, 0:CONV_PAD, :] = ubuf[:, ts:ts + CONV_PAD, :]

    x = h_ref[...]
    xn = _rms(x, an_ref[...]).astype(BF16)
    ab = jnp.dot(xn, win_ref[...], preferred_element_type=F32)
    u = ab[:, :c] * jax.nn.sigmoid(ab[:, c:])
    for rho in range(8):
        ubuf[rho, CONV_PAD - rho:CONV_PAD - rho + ts, :] = u

    base_off = CONV_PAD - (CONV_WIDTH - 1)

    def chunk(rc, carry):
        r0 = pl.multiple_of(rc * CONV_ROWS, CONV_ROWS)
        for lc in range(c // CONV_LANES):
            lanes = slice(lc * CONV_LANES, (lc + 1) * CONV_LANES)
            acc = jnp.broadcast_to(dwb_ref[:, lanes], (CONV_ROWS, CONV_LANES))
            for rho in range(8):
                taps = [k for k in range(CONV_WIDTH) if (base_off + k) % 8 == rho]
                first = base_off + taps[0] - rho
                nrows = CONV_ROWS + (taps[-1] - taps[0])
                blk = ubuf[rho, pl.ds(r0 + first, nrows), lanes]
                for k in taps:
                    a = k - taps[0]
                    acc = acc + dw_ref[k:k + 1, lanes] * blk[a:a + CONV_ROWS, :]
            ybuf[pl.ds(r0, CONV_ROWS), lanes] = acc
        return carry

    lax.fori_loop(0, ts // CONV_ROWS, chunk, 0)

    y = ybuf[...]
    yn = _rms(y, cn_ref[...])
    act = (yn * jax.nn.sigmoid(yn)).astype(BF16)
    o_ref[...] = x + jnp.dot(act, wout_ref[...], preferred_element_type=F32)


def _conv_layer(h, a_norm, w_in, dw, dw_bias, conv_norm, w_out, *, batch, ts=256):
    n, d = h.shape
    c = w_out.shape[0]
    seq = n // batch
    nts = seq // ts
    dwp = jnp.zeros((CONV_PAD, c), F32).at[:CONV_WIDTH].set(dw)
    row = lambda b, t: (b * nts + t, 0)
    const = lambda b, t: (0, 0)
    return pl.pallas_call(
        _conv_layer_kernel,
        out_shape=jax.ShapeDtypeStruct((n, d), F32),
        grid=(batch, nts),
        in_specs=[
            pl.BlockSpec((ts, d), row),
            pl.BlockSpec((1, d), const),
            pl.BlockSpec((d, 2 * c), const),
            pl.BlockSpec((CONV_PAD, c), const),
            pl.BlockSpec((1, c), const),
            pl.BlockSpec((1, c), const),
            pl.BlockSpec((c, d), const),
        ],
        out_specs=pl.BlockSpec((ts, d), row),
        scratch_shapes=[pltpu.VMEM((8, ts + CONV_PAD, c), F32), pltpu.VMEM((ts, c), F32)],
        compiler_params=_cparams(("arbitrary", "arbitrary")),
        name="conv_layer",
    )(h, a_norm.reshape(1, d), w_in.astype(BF16), dwp, dw_bias.reshape(1, c),
      conv_norm.reshape(1, c), w_out.astype(BF16))


N_EXPERTS = 16
N_GROUPS = 4
GROUP_SIZE = N_EXPERTS // N_GROUPS
PAIRS = ((0, 1), (0, 2), (0, 3), (1, 2), (1, 3), (2, 3))
N_BUCKETS = N_GROUPS * len(PAIRS)
BUCKET_ROWS = 32
LANES = 128
NEG_INF = float("-inf")


def _split_bf16(x):
    hi = x.astype(BF16)
    lo = (x - hi.astype(F32)).astype(BF16)
    return hi, lo


def _router_logits(xn, wr_ref):
    hi, lo = _split_bf16(xn)
    w = wr_ref[...]
    both = jnp.dot(hi, w, preferred_element_type=F32)
    low = jnp.dot(lo, w[:, :LANES], preferred_element_type=F32)
    return both[:, :LANES] + both[:, LANES:] + low, hi


def _route_kernel(bias_ref, h_ref, fn_ref, wr_ref, tri_ref, bucket_ref, rank_ref, counts_ref, carry):
    tr = h_ref.shape[0]

    @pl.when(pl.program_id(0) == 0)
    def _():
        carry[...] = jnp.zeros_like(carry)

    xn = _rms(h_ref[...], fn_ref[...])
    logits, _ = _router_logits(xn, wr_ref)
    lt = logits.T
    score = jax.nn.sigmoid(lt[0:N_EXPERTS, :])
    biased = [score[e:e + 1, :] + bias_ref[e] for e in range(N_EXPERTS)]

    best = None
    for g in range(N_GROUPS):
        v = biased[GROUP_SIZE * g:GROUP_SIZE * (g + 1)]
        gs = None
        for a, b in PAIRS:
            gs = v[a] + v[b] if gs is None else jnp.maximum(gs, v[a] + v[b])
        if best is None:
            best, gsel = gs, jnp.zeros_like(gs, dtype=jnp.int32)
        else:
            better = gs > best
            gsel = jnp.where(better, g, gsel)
            best = jnp.where(better, gs, best)

    vb = []
    for i in range(GROUP_SIZE):
        val = biased[i]
        for g in range(1, N_GROUPS):
            val = jnp.where(gsel == g, biased[GROUP_SIZE * g + i], val)
        vb.append(val)

    def first_argmax(vals):
        m, idx = vals[0], jnp.zeros_like(gsel)
        for i in range(1, GROUP_SIZE):
            better = vals[i] > m
            idx = jnp.where(better, i, idx)
            m = jnp.where(better, vals[i], m)
        return idx

    i1 = first_argmax(vb)
    i2 = first_argmax([jnp.where(i1 == i, NEG_INF, vb[i]) for i in range(GROUP_SIZE)])
    lo = jnp.minimum(i1, i2)
    hi = jnp.maximum(i1, i2)
    pair = jnp.where(lo == 0, 0, jnp.where(lo == 1, 3, 5)) + hi - lo - 1
    bucket = gsel * len(PAIRS) + pair

    onehot = lax.broadcasted_iota(jnp.int32, (BUCKET_ROWS, tr), 0) == bucket
    before = jnp.dot(onehot.astype(BF16), tri_ref[...], preferred_element_type=F32)
    onef = onehot.astype(F32)
    rank = jnp.sum(onef * (before + carry[:, 0:1]), axis=0, keepdims=True)
    carry[...] = carry[...] + jnp.sum(onef, axis=1, keepdims=True)
    bucket_ref[0] = bucket
    rank_ref[0] = rank.astype(jnp.int32)
    counts_ref[...] = carry[...]


def _moe_route(h, fn, wr, bias, *, tr=512):
    n, d = h.shape
    nt = n // tr
    tri = jnp.triu(jnp.ones((tr, tr), F32), 1).astype(BF16)
    const = lambda i, *_: (0, 0)
    bucket, rank, counts = pl.pallas_call(
        _route_kernel,
        out_shape=(jax.ShapeDtypeStruct((nt, 1, tr), jnp.int32),
                   jax.ShapeDtypeStruct((nt, 1, tr), jnp.int32),
                   jax.ShapeDtypeStruct((BUCKET_ROWS, LANES), F32)),
        grid_spec=pltpu.PrefetchScalarGridSpec(
            num_scalar_prefetch=1,
            grid=(nt,),
            in_specs=[
                pl.BlockSpec((tr, d), lambda i, *_: (i, 0)),
                pl.BlockSpec((1, d), const),
                pl.BlockSpec((d, 2 * LANES), const),
                pl.BlockSpec((tr, tr), const),
            ],
            out_specs=[
                pl.BlockSpec((1, 1, tr), lambda i, *_: (i, 0, 0)),
                pl.BlockSpec((1, 1, tr), lambda i, *_: (i, 0, 0)),
                pl.BlockSpec((BUCKET_ROWS, LANES), const),
            ],
            scratch_shapes=[pltpu.VMEM((BUCKET_ROWS, LANES), F32)],
        ),
        compiler_params=_cparams(("arbitrary",)),
        name="moe_route",
    )(bias, h, fn, wr, tri)
    return bucket, rank, counts


def _scatter_kernel(offs_ref, bucket_ref, rank_ref, src_ref, init_ref, dst_ref, sem, *, ch):
    del init_ref
    for j in range(ch):
        row = offs_ref[bucket_ref[0, 0, j]] + rank_ref[0, 0, j]
        pltpu.make_async_copy(src_ref.at[pl.ds(j, 1)], dst_ref.at[pl.ds(row, 1)], sem).start(priority=j % 2)
    pltpu.make_async_copy(src_ref, dst_ref.at[pl.ds(0, ch)], sem).wait()


def _gather_kernel(offs_ref, bucket_ref, rank_ref, src_ref, dst_ref, sem, *, ch):
    for j in range(ch):
        row = offs_ref[bucket_ref[0, 0, j]] + rank_ref[0, 0, j]
        pltpu.make_async_copy(src_ref.at[pl.ds(row, 1)], dst_ref.at[pl.ds(j, 1)], sem).start(priority=j % 2)
    pltpu.make_async_copy(src_ref.at[pl.ds(0, ch)], dst_ref, sem).wait()


def _permute_rows(src, offs, bucket, rank, *, n_out, scatter, ch=256):
    n = bucket.size
    d = src.shape[1]
    bucket = bucket.reshape(n // ch, 1, ch)
    rank = rank.reshape(n // ch, 1, ch)
    smem_blk = pl.BlockSpec((1, 1, ch), lambda i, *_: (i, 0, 0), memory_space=pltpu.SMEM)
    any_spec = pl.BlockSpec(memory_space=pl.ANY)
    tile_spec = pl.BlockSpec((ch, d), lambda i, *_: (i, 0))
    if scatter:
        body, args, in_specs, out_spec, aliases = (
            _scatter_kernel, [src, jnp.zeros((n_out, d), src.dtype)], [tile_spec, any_spec], any_spec, {4: 0})
    else:
        body, args, in_specs, out_spec, aliases = _gather_kernel, [src], [any_spec], tile_spec, {}
    return pl.pallas_call(
        functools.partial(body, ch=ch),
        out_shape=jax.ShapeDtypeStruct((n_out, d), src.dtype),
        grid_spec=pltpu.PrefetchScalarGridSpec(
            num_scalar_prefetch=1,
            grid=(n // ch,),
            in_specs=[smem_blk, smem_blk] + in_specs,
            out_specs=out_spec,
            scratch_shapes=[pltpu.SemaphoreType.DMA(())],
        ),
        input_output_aliases=aliases,
        compiler_params=_cparams(("arbitrary",)),
        name="moe_scatter" if scatter else "moe_gather",
    )(offs, bucket, rank, *args)


def _ffn_kernel(ea_ref, eb_ref, nv_ref, hs_ref, fn_ref, wr_ref, w1a, w3a, w2a, w1b, w3b, w2b, o_ref):
    i = pl.program_id(0)

    @pl.when(nv_ref[i] == 0)
    def _():
        o_ref[...] = hs_ref[...]

    @pl.when(nv_ref[i] > 0)
    def _():
        x = hs_ref[...]
        xn = _rms(x, fn_ref[...])
        logits, xb = _router_logits(xn, wr_ref)
        score = jax.nn.sigmoid(logits)
        lane = lax.broadcasted_iota(jnp.int32, score.shape, 1)
        sa = jnp.sum(jnp.where(lane == ea_ref[i], score, 0.0), axis=-1, keepdims=True)
        sb = jnp.sum(jnp.where(lane == eb_ref[i], score, 0.0), axis=-1, keepdims=True)
        tot = sa + sb

        def expert(w1, w3, w2, gate):
            h1 = jnp.dot(xb, w1[...], preferred_element_type=F32)
            h3 = jnp.dot(xb, w3[...], preferred_element_type=F32)
            act = (h1 * jax.nn.sigmoid(h1)) * h3 * gate
            return jnp.dot(act.astype(BF16), w2[...], preferred_element_type=F32)

        o_ref[...] = x + expert(w1a, w3a, w2a, sa / tot) + expert(w1b, w3b, w2b, sb / tot)


def _moe_ffn(hs, fn, wr, w1, w3, w2, tile_ea, tile_eb, tile_nv, *, tm):
    ns, d = hs.shape
    de = w1.shape[2]
    ntiles = ns // tm
    const = lambda i, *_: (0, 0)
    wa = lambda i, ea, eb, nv: (ea[i], 0, 0)
    wb = lambda i, ea, eb, nv: (eb[i], 0, 0)
    return pl.pallas_call(
        _ffn_kernel,
        out_shape=jax.ShapeDtypeStruct((ns, d), F32),
        grid_spec=pltpu.PrefetchScalarGridSpec(
            num_scalar_prefetch=3,
            grid=(ntiles,),
            in_specs=[
                pl.BlockSpec((tm, d), lambda i, *_: (i, 0)),
                pl.BlockSpec((1, d), const),
                pl.BlockSpec((d, 2 * LANES), const),
                pl.BlockSpec((None, d, de), wa),
                pl.BlockSpec((None, d, de), wa),
                pl.BlockSpec((None, de, d), wa),
                pl.BlockSpec((None, d, de), wb),
                pl.BlockSpec((None, d, de), wb),
                pl.BlockSpec((None, de, d), wb),
            ],
            out_specs=pl.BlockSpec((tm, d), lambda i, *_: (i, 0)),
        ),
        compiler_params=_cparams(("arbitrary",)),
        name="moe_ffn",
    )(tile_ea, tile_eb, tile_nv, hs, fn, wr, w1, w3, w2, w1, w3, w2)


def _router_weights(router_w):
    d = router_w.shape[0]
    hi, lo = _split_bf16(router_w)
    wr = jnp.zeros((d, 2 * LANES), BF16)
    return wr.at[:, :N_EXPERTS].set(hi).at[:, LANES:LANES + N_EXPERTS].set(lo)


def _moe_layer(h, fn, wr, bias, w1, w3, w2, *, tm=512):
    n, d = h.shape
    fn = fn.reshape(1, d)
    bucket, rank, counts = _moe_route(h, fn, wr, bias)
    counts = counts[:N_BUCKETS, 0].astype(jnp.int32)
    ntile_b = (counts + tm - 1) // tm
    tile_end = jnp.cumsum(ntile_b)
    tile_start = tile_end - ntile_b
    offs = jnp.zeros((BUCKET_ROWS,), jnp.int32).at[:N_BUCKETS].set(tile_start * tm)
    ntiles = n // tm + N_BUCKETS
    j = jnp.arange(ntiles, dtype=jnp.int32)
    used = j < tile_end[-1]
    tb = jnp.minimum(jnp.searchsorted(tile_end, j, side="right").astype(jnp.int32), N_BUCKETS - 1)
    last_b = jnp.max(jnp.where(ntile_b > 0, jnp.arange(N_BUCKETS, dtype=jnp.int32), 0))
    tb = jnp.where(used, tb, last_b)
    nv = jnp.where(used, jnp.clip(counts[tb] - (j - tile_start[tb]) * tm, 0, tm), 0).astype(jnp.int32)
    pair_lo = jnp.asarray([p[0] for p in PAIRS], jnp.int32)
    pair_hi = jnp.asarray([p[1] for p in PAIRS], jnp.int32)
    ea = GROUP_SIZE * (tb // len(PAIRS)) + pair_lo[tb % len(PAIRS)]
    eb = GROUP_SIZE * (tb // len(PAIRS)) + pair_hi[tb % len(PAIRS)]
    hs = _permute_rows(h, offs, bucket, rank, n_out=ntiles * tm, scatter=True)
    ys = _moe_ffn(hs, fn, wr, w1, w3, w2, ea, eb, nv, tm=tm)
    return _permute_rows(ys, offs, bucket, rank, n_out=n, scatter=False)


N_HEADS = 16
N_KV_GROUPS = 4
HEADS_PER_GROUP = N_HEADS // N_KV_GROUPS
HEAD_DIM = 64
CMP_LEN = 32
CMP_STRIDE = 16
SLC_LEN = 64
SLC_SHIFT = 6
SLC_TOP_N = 16
WINDOW = 512
N_BRANCH = 3
MASKED = -1e30
NT_DIMS = (((1,), (1,)), ((), ()))


def _segment_matrix():
    seg = np.kron(np.eye(LANES // HEAD_DIM), np.full((HEAD_DIM, HEAD_DIM), 1.0 / HEAD_DIM))
    return jnp.asarray(seg, BF16)


def _placement_matrix():
    p = np.zeros((LANES, 4 * LANES), np.float32)
    for half in range(2):
        for l in range(HEAD_DIM):
            p[half * HEAD_DIM + l, half * 2 * LANES + l] = 1.0
            p[half * HEAD_DIM + l, half * 2 * LANES + 3 * HEAD_DIM + l] = 1.0
    return jnp.asarray(p, BF16)


def _seg_rms(x, seg, gain):
    hi, lo = _split_bf16(x * x)
    ms = jnp.dot(hi, seg, preferred_element_type=F32) + jnp.dot(lo, seg, preferred_element_type=F32)
    return x * lax.rsqrt(ms + EPS) * gain


def _kv_proj_kernel(x_ref, kvn_ref, wkv_ref, seg_ref, pslab_ref, kgain_ref,
                    kcvc_ref, ks_ref, vs_ref, kw_ref, vw_ref):
    xn = _rms(x_ref[...], kvn_ref[...]).astype(BF16)
    kv = jnp.dot(xn, wkv_ref[...], preferred_element_type=F32)
    width = N_KV_GROUPS * HEAD_DIM
    for s in range(4):
        kcvc_ref[s] = kv[:, LANES * s:LANES * (s + 1)]
    seg = seg_ref[...]
    pslab = pslab_ref[...]

    def place(i, out_ref, gain_row):
        for s in range(width // LANES):
            slab = kv[:, width * i + LANES * s:width * i + LANES * (s + 1)]
            if gain_row is not None:
                slab = _seg_rms(slab, seg, kgain_ref[gain_row:gain_row + 1, :])
            out_ref[:, 4 * LANES * s:4 * LANES * (s + 1)] = jnp.dot(
                slab.astype(BF16), pslab, preferred_element_type=F32).astype(BF16)

    place(2, ks_ref, 1)
    place(3, vs_ref, None)
    place(4, kw_ref, 2)
    place(5, vw_ref, None)


def _kv_proj(x, kv_norm, w_kv, k_norm, *, t=512):
    n, d = x.shape
    width = N_KV_GROUPS * HEAD_DIM
    kgain = jnp.zeros((8, LANES), F32).at[:N_BRANCH].set(jnp.tile(k_norm, (1, LANES // HEAD_DIM)))
    const = lambda i: (0, 0)
    wide = jax.ShapeDtypeStruct((n, 4 * width), BF16)
    wide_spec = pl.BlockSpec((t, 4 * width), lambda i: (i, 0))
    return pl.pallas_call(
        _kv_proj_kernel,
        out_shape=(jax.ShapeDtypeStruct((4, n, LANES), F32), wide, wide, wide, wide),
        grid=(n // t,),
        in_specs=[
            pl.BlockSpec((t, d), lambda i: (i, 0)),
            pl.BlockSpec((1, d), const),
            pl.BlockSpec((d, 6 * width), const),
            pl.BlockSpec((LANES, LANES), const),
            pl.BlockSpec((LANES, 4 * LANES), const),
            pl.BlockSpec((8, LANES), const),
        ],
        out_specs=(pl.BlockSpec((4, t, LANES), lambda i: (0, i, 0)), wide_spec, wide_spec, wide_spec, wide_spec),
        compiler_params=_cparams(("arbitrary",)),
        name="nsa_kv_proj",
    )(x, kv_norm.reshape(1, d), w_kv.astype(BF16), _segment_matrix(), _placement_matrix(), kgain)


def _compress_kernel(x_ref, pe_ref, w1_ref, b1_ref, w2_ref, seg_ref, pslab_ref, gain_ref, o_ref, bm_scr,
                     *, normalize):
    seq = x_ref.shape[1]
    nc = seq // CMP_STRIDE
    halves = CMP_LEN // CMP_STRIDE
    assert halves == 2
    hidden = w1_ref.shape[2]
    for slab in range(2):
        first = jnp.zeros((nc, hidden), F32)
        second = jnp.zeros((nc, hidden), F32)
        for j in range(CMP_STRIDE):
            xj = x_ref[slab, pl.ds(j, nc, stride=CMP_STRIDE), :]
            first += jnp.dot((xj + pe_ref[j:j + 1, :]).astype(BF16), w1_ref[j], preferred_element_type=F32)
            second += jnp.dot((xj + pe_ref[CMP_STRIDE + j:CMP_STRIDE + j + 1, :]).astype(BF16),
                              w1_ref[CMP_STRIDE + j], preferred_element_type=F32)
        bm_scr[0:nc, :] = second
        bm_scr[nc:nc + 8, :] = jnp.zeros((8, hidden), F32)
        z = first + bm_scr[1:nc + 1, :] + b1_ref[...]
        hid = jax.nn.gelu(z)
        out = jnp.dot(hid.astype(BF16), w2_ref[...], preferred_element_type=F32)
        if normalize:
            out = _seg_rms(out, seg_ref[...], gain_ref[0:1, :])
        out = jnp.where(lax.broadcasted_iota(jnp.int32, out.shape, 0) < nc - 1, out, 0.0)
        o_ref[:, 4 * LANES * slab:4 * LANES * (slab + 1)] = jnp.dot(
            out.astype(BF16), pslab_ref[...], preferred_element_type=F32).astype(BF16)


def _compress(kcvc, which, pe, w1, b1, w2, gain, *, batch, normalize):
    n = kcvc.shape[1]
    seq = n // batch
    nc = seq // CMP_STRIDE
    hidden = w1.shape[1]
    eye2 = jnp.eye(2, dtype=F32)
    w1j = w1.reshape(CMP_LEN, HEAD_DIM, hidden)
    w1bd = jax.vmap(lambda m: jnp.kron(eye2, m))(w1j).astype(BF16)
    w2bd = jnp.kron(eye2, w2).astype(BF16)
    pe2 = jnp.tile(pe, (1, 2))
    b12 = jnp.tile(b1.reshape(1, hidden), (1, 2))
    gain2 = jnp.zeros((8, LANES), F32).at[0].set(jnp.tile(gain, 2))
    const2 = lambda b: (0, 0)
    return pl.pallas_call(
        functools.partial(_compress_kernel, normalize=normalize),
        out_shape=jax.ShapeDtypeStruct((batch, nc, 8 * LANES), BF16),
        grid=(batch,),
        in_specs=[
            pl.BlockSpec((2, seq, LANES), lambda b: (which, b, 0)),
            pl.BlockSpec((CMP_LEN, LANES), const2),
            pl.BlockSpec((CMP_LEN, LANES, 2 * hidden), lambda b: (0, 0, 0)),
            pl.BlockSpec((1, 2 * hidden), const2),
            pl.BlockSpec((2 * hidden, LANES), const2),
            pl.BlockSpec((LANES, LANES), const2),
            pl.BlockSpec((LANES, 4 * LANES), const2),
            pl.BlockSpec((8, LANES), const2),
        ],
        out_specs=pl.BlockSpec((None, nc, 8 * LANES), lambda b: (b, 0, 0)),
        scratch_shapes=[pltpu.VMEM((nc + 8, 2 * hidden), F32)],
        compiler_params=_cparams(("arbitrary",)),
        name="nsa_compress",
    )(kcvc, pe2, w1bd, b12, w2bd, _segment_matrix(), _placement_matrix(), gain2)


def _nsa_proj_kernel(h_ref, bn_ref, win_ref, seg_ref, qgain_ref, q_ref, g_ref):
    xn = _rms(h_ref[...], bn_ref[...]).astype(BF16)
    proj = jnp.dot(xn, win_ref[...], preferred_element_type=F32)
    nq = q_ref.shape[1]
    seg = seg_ref[...]
    for s in range(nq // LANES):
        slab = proj[:, LANES * s:LANES * (s + 1)]
        q_ref[:, LANES * s:LANES * (s + 1)] = _seg_rms(slab, seg, qgain_ref[...]).astype(BF16)
    g_ref[...] = jax.nn.sigmoid(proj[:, nq:nq + LANES])


def _nsa_proj(h, b_norm, w_in, q_norm, *, t=512):
    n, d = h.shape
    nq = N_HEADS * HEAD_DIM
    win = jnp.zeros((d, nq + LANES), BF16).at[:, :w_in.shape[1]].set(w_in.astype(BF16))
    qgain = jnp.tile(q_norm * HEAD_DIM ** -0.5, LANES // HEAD_DIM).reshape(1, LANES)
    const = lambda i: (0, 0)
    return pl.pallas_call(
        _nsa_proj_kernel,
        out_shape=(jax.ShapeDtypeStruct((n, nq), BF16), jax.ShapeDtypeStruct((n, LANES), F32)),
        grid=(n // t,),
        in_specs=[
            pl.BlockSpec((t, d), lambda i: (i, 0)),
            pl.BlockSpec((1, d), const),
            pl.BlockSpec((d, nq + LANES), const),
            pl.BlockSpec((LANES, LANES), const),
            pl.BlockSpec((1, LANES), const),
        ],
        out_specs=(pl.BlockSpec((t, nq), lambda i: (i, 0)), pl.BlockSpec((t, LANES), lambda i: (i, 0))),
        compiler_params=_cparams(("arbitrary",)),
        name="nsa_proj",
    )(h, b_norm.reshape(1, d), win, _segment_matrix(), qgain)


def _masked_softmax(s, valid):
    sm = jnp.where(valid, s, MASKED)
    m = jnp.max(sm, axis=-1, keepdims=True)
    e = jnp.where(valid, jnp.exp(sm - m), 0.0)
    tot = jnp.sum(e, axis=-1, keepdims=True)
    return e / jnp.where(tot > 0.0, tot, 1.0)


def _nsa_attn_kernel(q_ref, g_ref, kc_ref, vc_ref, ks_ref, vs_ref, kw_ref, vw_ref, mapm_ref, gsel_ref,
                     o_ref, me_scr, mo_scr, le_scr, lo_scr, acc_scr, *, tq, tk):
    q0 = pl.program_id(2) * tq
    seq = ks_ref.shape[0]
    qq = jnp.concatenate([q_ref[:, 0:LANES], q_ref[:, LANES:2 * LANES]], axis=0)
    t = q0 + lax.broadcasted_iota(jnp.int32, (tq, 1), 0)
    t2 = jnp.concatenate([t, t], axis=0)
    lane_lo = lax.broadcasted_iota(jnp.int32, (2 * tq, LANES), 1) < HEAD_DIM

    def scores(k_blk):
        return (lax.dot_general(qq, k_blk[:, :LANES], NT_DIMS, preferred_element_type=F32),
                lax.dot_general(qq, k_blk[:, LANES:], NT_DIMS, preferred_element_type=F32))

    def weighted(p_even, p_odd, v_blk):
        return (jnp.dot(p_even.astype(BF16), v_blk[:, :LANES], preferred_element_type=F32)
                + jnp.dot(p_odd.astype(BF16), v_blk[:, LANES:], preferred_element_type=F32))

    kc = kc_ref[...]
    nc = kc.shape[0]
    s_e, s_o = scores(kc)
    cmp_end = lax.broadcasted_iota(jnp.int32, (1, nc), 1) * CMP_STRIDE + (CMP_LEN - 1)
    valid_c = cmp_end <= t2
    p_e = _masked_softmax(s_e, valid_c)
    p_o = _masked_softmax(s_o, valid_c)
    o_cmp = weighted(p_e, p_o, vc_ref[...])
    p_sum = (p_e[:tq] + p_o[:tq]) + (p_e[tq:] + p_o[tq:])
    hi, lo = _split_bf16(p_sum)
    mapm = mapm_ref[...]
    imp = jnp.dot(hi, mapm, preferred_element_type=F32) + jnp.dot(lo, mapm, preferred_element_type=F32)

    blk = lax.broadcasted_iota(jnp.int32, (tq, LANES), 1)
    cur = jnp.right_shift(t, SLC_SHIFT)
    forced = (blk == 0) | (blk == cur) | (blk == cur - 1)
    work = jnp.where(forced, jnp.inf, imp)
    work = jnp.where(blk > cur, NEG_INF, work)
    sel = jnp.zeros((tq, LANES), F32)
    for _ in range(SLC_TOP_N):
        m = jnp.max(work, axis=-1, keepdims=True)
        idx = jnp.min(jnp.where(work == m, blk, LANES), axis=-1, keepdims=True)
        pick = blk == idx
        sel = jnp.where(pick, 1.0, sel)
        work = jnp.where(pick, NEG_INF, work)
    sel2 = jnp.concatenate([sel, sel], axis=0).astype(BF16)

    me_scr[...] = jnp.full(me_scr.shape, MASKED, F32)
    mo_scr[...] = jnp.full(mo_scr.shape, MASKED, F32)
    le_scr[...] = jnp.zeros(le_scr.shape, F32)
    lo_scr[...] = jnp.zeros(lo_scr.shape, F32)
    acc_scr[...] = jnp.zeros(acc_scr.shape, F32)

    def key_tile(kt, carry):
        k0 = pl.multiple_of(kt * tk, tk)
        kpos = k0 + lax.broadcasted_iota(jnp.int32, (1, tk), 1)
        expand = (lax.broadcasted_iota(jnp.int32, (LANES, tk), 0)
                  == jnp.right_shift(k0 + lax.broadcasted_iota(jnp.int32, (LANES, tk), 1), SLC_SHIFT))
        chosen = jnp.dot(sel2, jnp.where(expand, 1.0, 0.0).astype(BF16), preferred_element_type=F32)
        mask = (chosen > 0.5) & (kpos <= t2)
        s_e, s_o = scores(ks_ref[pl.ds(k0, tk), :])

        def update(s, m_scr, l_scr):
            m_old = m_scr[...]
            m_new = jnp.maximum(m_old, jnp.max(jnp.where(mask, s, MASKED), axis=-1, keepdims=True))
            p = jnp.where(mask, jnp.exp(s - m_new[:, 0:1]), 0.0)
            alpha = jnp.exp(m_old - m_new)
            l_scr[...] = alpha * l_scr[...] + jnp.sum(p, axis=-1, keepdims=True)
            m_scr[...] = m_new
            return p, alpha

        p_e, a_e = update(s_e, me_scr, le_scr)
        p_o, a_o = update(s_o, mo_scr, lo_scr)
        acc_scr[...] = (acc_scr[...] * jnp.where(lane_lo, a_e, a_o)
                        + weighted(p_e, p_o, vs_ref[pl.ds(k0, tk), :]))
        return carry

    lax.fori_loop(0, (q0 + tq + tk - 1) // tk, key_tile, 0)
    o_sel = acc_scr[...] / jnp.where(lane_lo, le_scr[...], lo_scr[...])

    span = WINDOW + tq
    w0 = pl.multiple_of(jnp.clip(q0 - WINDOW, 0, seq - span), tq)
    wpos = w0 + lax.broadcasted_iota(jnp.int32, (1, span), 1)
    valid_w = (wpos <= t2) & (wpos > t2 - WINDOW)
    s_e, s_o = scores(kw_ref[pl.ds(w0, span), :])
    o_win = weighted(_masked_softmax(s_e, valid_w), _masked_softmax(s_o, valid_w), vw_ref[pl.ds(w0, span), :])

    gh, gl = _split_bf16(g_ref[...])
    gsel = gsel_ref[...]
    gx = jnp.dot(gh, gsel, preferred_element_type=F32) + jnp.dot(gl, gsel, preferred_element_type=F32)
    out = jnp.zeros((2 * tq, LANES), F32)
    for br, o_br in enumerate((o_cmp, o_sel, o_win)):
        gate = jnp.concatenate([gx[:, LANES * br:LANES * (br + 1)],
                                gx[:, LANES * (N_BRANCH + br):LANES * (N_BRANCH + br + 1)]], axis=0)
        out = out + gate * o_br
    o_ref[:, 0:LANES] = out[:tq].astype(o_ref.dtype)
    o_ref[:, LANES:2 * LANES] = out[tq:].astype(o_ref.dtype)


def _importance_map(nc, n_cmp):
    ratio, span = SLC_LEN // CMP_STRIDE, CMP_LEN // CMP_STRIDE
    w = np.convolve(np.ones(ratio), np.ones(span))
    m = np.zeros((nc, LANES), np.float32)
    for j in range(LANES):
        for o, wt in enumerate(w):
            c = ratio * j + o - (span - 1)
            if 0 <= c < n_cmp:
                m[c, j] = wt
    return jnp.asarray(m, BF16)


def _gate_select():
    m = np.zeros((N_KV_GROUPS, LANES, 2 * N_BRANCH * LANES), np.float32)
    for g in range(N_KV_GROUPS):
        for pair in range(2):
            for br in range(N_BRANCH):
                for lane in range(LANES):
                    r = 2 * pair + lane // HEAD_DIM
                    m[g, (g * HEADS_PER_GROUP + r) * N_BRANCH + br, (pair * N_BRANCH + br) * LANES + lane] = 1.0
    return jnp.asarray(m, BF16)


def _nsa_attention(q, gates, shared, *, batch, tq=256, tk=512):
    kcmp, vcmp, ks, vs, kw, vw = shared
    n = q.shape[0]
    seq = n // batch
    nq = seq // tq
    nc = kcmp.shape[1]
    width = 2 * LANES
    ks, vs, kw, vw = (a.reshape(batch, seq, N_KV_GROUPS * width) for a in (ks, vs, kw, vw))
    qrow = lambda b, g, i: (b * nq + i, g)
    per_bg = lambda b, g, i: (b, 0, g)
    stat = pltpu.VMEM((2 * tq, LANES), F32)
    return pl.pallas_call(
        functools.partial(_nsa_attn_kernel, tq=tq, tk=tk),
        out_shape=jax.ShapeDtypeStruct((n, N_HEADS * HEAD_DIM), BF16),
        grid=(batch, N_KV_GROUPS, nq),
        in_specs=[
            pl.BlockSpec((tq, width), qrow),
            pl.BlockSpec((tq, LANES), lambda b, g, i: (b * nq + i, 0)),
            pl.BlockSpec((None, nc, width), per_bg),
            pl.BlockSpec((None, nc, width), per_bg),
            pl.BlockSpec((None, seq, width), per_bg),
            pl.BlockSpec((None, seq, width), per_bg),
            pl.BlockSpec((None, seq, width), per_bg),
            pl.BlockSpec((None, seq, width), per_bg),
            pl.BlockSpec((nc, LANES), lambda b, g, i: (0, 0)),
            pl.BlockSpec((None, LANES, 2 * N_BRANCH * LANES), lambda b, g, i: (g, 0, 0)),
        ],
        out_specs=pl.BlockSpec((tq, width), qrow),
        scratch_shapes=[stat, stat, stat, stat, stat],
        compiler_params=_cparams(("arbitrary", "arbitrary", "arbitrary")),
        name="nsa_attention",
    )(q, gates, kcmp, vcmp, ks, vs, kw, vw, _importance_map(nc, (seq - CMP_LEN) // CMP_STRIDE + 1), _gate_select())


def _out_proj_kernel(h_ref, o_ref, w_ref, out_ref):
    out_ref[...] = h_ref[...] + jnp.dot(o_ref[...], w_ref[...], preferred_element_type=F32)


def _out_proj(h, o, w_out, *, t=512):
    n, d = h.shape
    k = o.shape[1]
    return pl.pallas_call(
        _out_proj_kernel,
        out_shape=jax.ShapeDtypeStruct((n, d), F32),
        grid=(n // t,),
        in_specs=[pl.BlockSpec((t, d), lambda i: (i, 0)), pl.BlockSpec((t, k), lambda i: (i, 0)),
                  pl.BlockSpec((k, d), lambda i: (0, 0))],
        out_specs=pl.BlockSpec((t, d), lambda i: (i, 0)),
        compiler_params=_cparams(("arbitrary",)),
        name="nsa_out_proj",
    )(h, o, w_out.astype(BF16))


def _nsa_shared_kv(x, kv_norm, w_kv, cmp_pe, phi_w1, phi_b1, phi_w2, k_norm, *, batch):
    kcvc, ks, vs, kw, vw = _kv_proj(x, kv_norm, w_kv, k_norm)
    kcmp = _compress(kcvc, 0, cmp_pe[0], phi_w1[0], phi_b1[0], phi_w2[0], k_norm[0], batch=batch, normalize=True)
    vcmp = _compress(kcvc, 1, cmp_pe[1], phi_w1[1], phi_b1[1], phi_w2[1], k_norm[0], batch=batch, normalize=False)
    return kcmp, vcmp, ks, vs, kw, vw


def _nsa_layer(h, b_norm, w_in, q_norm, w_out, shared, *, batch):
    q, gates = _nsa_proj(h, b_norm, w_in, q_norm)
    o = _nsa_attention(q, gates, shared, batch=batch)
    return _out_proj(h, o, w_out)


def kernel(x, a_norm, a_w_in, a_dw, a_dw_bias, a_conv_norm, a_w_out, kv_norm, w_kv, cmp_pe, phi_w1, phi_b1, phi_w2, k_norm, b_norm, b_w_in, b_q_norm, b_w_out, ffn_norm, router_w, router_bias, moe_w1, moe_w3, moe_w2):
    bsz, s, d = x.shape
    h = x.reshape(bsz * s, d)
    wr = _router_weights(router_w)
    n_a = a_norm.shape[0]
    depth = ffn_norm.shape[0]
    shared = None
    for layer in range(depth):
        if layer < n_a:
            i = layer
            h = _conv_layer(h, a_norm[i], a_w_in[i], a_dw[i], a_dw_bias[i], a_conv_norm[i], a_w_out[i], batch=bsz)
        else:
            i = layer - n_a
            h = _nsa_layer(h, b_norm[i], b_w_in[i], b_q_norm[i], b_w_out[i], shared, batch=bsz)
        h = _moe_layer(h, ffn_norm[layer], wr, router_bias, moe_w1[layer].astype(BF16),
                       moe_w3[layer].astype(BF16), moe_w2[layer].astype(BF16))
        if layer == n_a - 1:
            shared = _nsa_shared_kv(h, kv_norm, w_kv, cmp_pe, phi_w1, phi_b1, phi_w2, k_norm, batch=bsz)
    return h.reshape(bsz, s, d)
```

```python
import functools

import jax
import jax.numpy as jnp
import numpy as np
from jax import lax
from jax.experimental import pallas as pl
from jax.experimental.pallas import tpu as pltpu

F32 = jnp.float32
BF16 = jnp.bfloat16

EPS = 1e-6
CONV_WIDTH = 31
CONV_PAD = 32
VMEM_LIMIT = 56 * 1024 * 1024


def _rms(x, g):
    ms = jnp.mean(x * x, axis=-1, keepdims=True)
    return x * lax.rsqrt(ms + EPS) * g


def _cparams(sem):
    return pltpu.CompilerParams(dimension_semantics=sem, vmem_limit_bytes=VMEM_LIMIT)


CONV_ROWS = 64
CONV_LANES = 256


def _conv_layer_kernel(h_ref, an_ref, win_ref, dw_ref, dwb_ref, cn_ref, wout_ref, o_ref, ubuf, ybuf):
    ts, c = ybuf.shape
    t = pl.program_id(1)

    @pl.when(t == 0)
    def _():
        ubuf[:, 0:CONV_PAD, :] = jnp.zeros((8, CONV_PAD, c), F32)

    @pl.when(t != 0)
    def _():
        ubuf[:, 0:CONV_PAD, :] = ubuf[:, ts:ts + CONV_PAD, :]

    x = h_ref[...]
    xn = _rms(x, an_ref[...]).astype(BF16)
    ab = jnp.dot(xn, win_ref[...], preferred_element_type=F32)
    u = ab[:, :c] * jax.nn.sigmoid(ab[:, c:])
    for rho in range(8):
        ubuf[rho, CONV_PAD - rho:CONV_PAD - rho + ts, :] = u

    base_off = CONV_PAD - (CONV_WIDTH - 1)

    def chunk(rc, carry):
        r0 = pl.multiple_of(rc * CONV_ROWS, CONV_ROWS)
        for lc in range(c // CONV_LANES):
            lanes = slice(lc * CONV_LANES, (lc + 1) * CONV_LANES)
            acc = jnp.broadcast_to(dwb_ref[:, lanes], (CONV_ROWS, CONV_LANES))
            for rho in range(8):
                taps = [k for k in range(CONV_WIDTH) if (base_off + k) % 8 == rho]
                first = base_off + taps[0] - rho
                nrows = CONV_ROWS + (taps[-1] - taps[0])
                blk = ubuf[rho, pl.ds(r0 + first, nrows), lanes]
                for k in taps:
                    a = k - taps[0]
                    acc = acc + dw_ref[k:k + 1, lanes] * blk[a:a + CONV_ROWS, :]
            ybuf[pl.ds(r0, CONV_ROWS), lanes] = acc
        return carry

    lax.fori_loop(0, ts // CONV_ROWS, chunk, 0)

    y = ybuf[...]
    yn = _rms(y, cn_ref[...])
    act = (yn * jax.nn.sigmoid(yn)).astype(BF16)
    o_ref[...] = x + jnp.dot(act, wout_ref[...], preferred_element_type=F32)


def _conv_layer(h, a_norm, w_in, dw, dw_bias, conv_norm, w_out, *, batch, ts=256):
    n, d = h.shape
    c = w_out.shape[0]
    seq = n // batch
    nts = seq // ts
    dwp = jnp.zeros((CONV_PAD, c), F32).at[:CONV_WIDTH].set(dw)
    row = lambda b, t: (b * nts + t, 0)
    const = lambda b, t: (0, 0)
    return pl.pallas_call(
        _conv_layer_kernel,
        out_shape=jax.ShapeDtypeStruct((n, d), F32),
        grid=(batch, nts),
        in_specs=[
            pl.BlockSpec((ts, d), row),
            pl.BlockSpec((1, d), const),
            pl.BlockSpec((d, 2 * c), const),
            pl.BlockSpec((CONV_PAD, c), const),
            pl.BlockSpec((1, c), const),
            pl.BlockSpec((1, c), const),
            pl.BlockSpec((c, d), const),
        ],
        out_specs=pl.BlockSpec((ts, d), row),
        scratch_shapes=[pltpu.VMEM((8, ts + CONV_PAD, c), F32), pltpu.VMEM((ts, c), F32)],
        compiler_params=_cparams(("arbitrary", "arbitrary")),
        name="conv_layer",
    )(h, a_norm.reshape(1, d), w_in.astype(BF16), dwp, dw_bias.reshape(1, c),
      conv_norm.reshape(1, c), w_out.astype(BF16))


N_EXPERTS = 16
N_GROUPS = 4
GROUP_SIZE = N_EXPERTS // N_GROUPS
PAIRS = ((0, 1), (0, 2), (0, 3), (1, 2), (1, 3), (2, 3))
N_BUCKETS = N_GROUPS * len(PAIRS)
BUCKET_ROWS = 32
LANES = 128
NEG_INF = float("-inf")


def _split_bf16(x):
    hi = x.astype(BF16)
    lo = (x - hi.astype(F32)).astype(BF16)
    return hi, lo


def _router_logits(xn, wr_ref):
    hi, lo = _split_bf16(xn)
    w = wr_ref[...]
    both = jnp.dot(hi, w, preferred_element_type=F32)
    low = jnp.dot(lo, w[:, :LANES], preferred_element_type=F32)
    return both[:, :LANES] + both[:, LANES:] + low, hi


def _route_kernel(bias_ref, h_ref, fn_ref, wr_ref, tri_ref, bucket_ref, rank_ref, counts_ref, carry):
    tr = h_ref.shape[0]

    @pl.when(pl.program_id(0) == 0)
    def _():
        carry[...] = jnp.zeros_like(carry)

    xn = _rms(h_ref[...], fn_ref[...])
    logits, _ = _router_logits(xn, wr_ref)
    lt = logits.T
    score = jax.nn.sigmoid(lt[0:N_EXPERTS, :])
    biased = [score[e:e + 1, :] + bias_ref[e] for e in range(N_EXPERTS)]

    best = None
    for g in range(N_GROUPS):
        v = biased[GROUP_SIZE * g:GROUP_SIZE * (g + 1)]
        gs = None
        for a, b in PAIRS:
            gs = v[a] + v[b] if gs is None else jnp.maximum(gs, v[a] + v[b])
        if best is None:
            best, gsel = gs, jnp.zeros_like(gs, dtype=jnp.int32)
        else:
            better = gs > best
            gsel = jnp.where(better, g, gsel)
            best = jnp.where(better, gs, best)

    vb = []
    for i in range(GROUP_SIZE):
        val = biased[i]
        for g in range(1, N_GROUPS):
            val = jnp.where(gsel == g, biased[GROUP_SIZE * g + i], val)
        vb.append(val)

    def first_argmax(vals):
        m, idx = vals[0], jnp.zeros_like(gsel)
        for i in range(1, GROUP_SIZE):
            better = vals[i] > m
            idx = jnp.where(better, i, idx)
            m = jnp.where(better, vals[i], m)
        return idx

    i1 = first_argmax(vb)
    i2 = first_argmax([jnp.where(i1 == i, NEG_INF, vb[i]) for i in range(GROUP_SIZE)])
    lo = jnp.minimum(i1, i2)
    hi = jnp.maximum(i1, i2)
    pair = jnp.where(lo == 0, 0, jnp.where(lo == 1, 3, 5)) + hi - lo - 1
    bucket = gsel * len(PAIRS) + pair

    onehot = lax.broadcasted_iota(jnp.int32, (BUCKET_ROWS, tr), 0) == bucket
    before = jnp.dot(onehot.astype(BF16), tri_ref[...], preferred_element_type=F32)
    onef = onehot.astype(F32)
    rank = jnp.sum(onef * (before + carry[:, 0:1]), axis=0, keepdims=True)
    carry[...] = carry[...] + jnp.sum(onef, axis=1, keepdims=True)
    bucket_ref[0] = bucket
    rank_ref[0] = rank.astype(jnp.int32)
    counts_ref[...] = carry[...]


def _moe_route(h, fn, wr, bias, *, tr=512):
    n, d = h.shape
    nt = n // tr
    tri = jnp.triu(jnp.ones((tr, tr), F32), 1).astype(BF16)
    const = lambda i, *_: (0, 0)
    bucket, rank, counts = pl.pallas_call(
        _route_kernel,
        out_shape=(jax.ShapeDtypeStruct((nt, 1, tr), jnp.int32),
                   jax.ShapeDtypeStruct((nt, 1, tr), jnp.int32),
                   jax.ShapeDtypeStruct((BUCKET_ROWS, LANES), F32)),
        grid_spec=pltpu.PrefetchScalarGridSpec(
            num_scalar_prefetch=1,
            grid=(nt,),
            in_specs=[
                pl.BlockSpec((tr, d), lambda i, *_: (i, 0)),
                pl.BlockSpec((1, d), const),
                pl.BlockSpec((d, 2 * LANES), const),
                pl.BlockSpec((tr, tr), const),
            ],
            out_specs=[
                pl.BlockSpec((1, 1, tr), lambda i, *_: (i, 0, 0)),
                pl.BlockSpec((1, 1, tr), lambda i, *_: (i, 0, 0)),
                pl.BlockSpec((BUCKET_ROWS, LANES), const),
            ],
            scratch_shapes=[pltpu.VMEM((BUCKET_ROWS, LANES), F32)],
        ),
        compiler_params=_cparams(("arbitrary",)),
        name="moe_route",
    )(bias, h, fn, wr, tri)
    return bucket, rank, counts


def _scatter_kernel(offs_ref, bucket_ref, rank_ref, src_ref, init_ref, dst_ref, sem, *, ch):
    del init_ref
    for j in range(ch):
        row = offs_ref[bucket_ref[0, 0, j]] + rank_ref[0, 0, j]
        pltpu.make_async_copy(src_ref.at[pl.ds(j, 1)], dst_ref.at[pl.ds(row, 1)], sem).start(priority=j % 2)
    pltpu.make_async_copy(src_ref, dst_ref.at[pl.ds(0, ch)], sem).wait()


def _gather_kernel(offs_ref, bucket_ref, rank_ref, src_ref, dst_ref, sem, *, ch):
    for j in range(ch):
        row = offs_ref[bucket_ref[0, 0, j]] + rank_ref[0, 0, j]
        pltpu.make_async_copy(src_ref.at[pl.ds(row, 1)], dst_ref.at[pl.ds(j, 1)], sem).start(priority=j % 2)
    pltpu.make_async_copy(src_ref.at[pl.ds(0, ch)], dst_ref, sem).wait()


def _permute_rows(src, offs, bucket, rank, *, n_out, scatter, ch=256):
    n = bucket.size
    d = src.shape[1]
    bucket = bucket.reshape(n // ch, 1, ch)
    rank = rank.reshape(n // ch, 1, ch)
    smem_blk = pl.BlockSpec((1, 1, ch), lambda i, *_: (i, 0, 0), memory_space=pltpu.SMEM)
    any_spec = pl.BlockSpec(memory_space=pl.ANY)
    tile_spec = pl.BlockSpec((ch, d), lambda i, *_: (i, 0))
    if scatter:
        body, args, in_specs, out_spec, aliases = (
            _scatter_kernel, [src, jnp.zeros((n_out, d), src.dtype)], [tile_spec, any_spec], any_spec, {4: 0})
    else:
        body, args, in_specs, out_spec, aliases = _gather_kernel, [src], [any_spec], tile_spec, {}
    return pl.pallas_call(
        functools.partial(body, ch=ch),
        out_shape=jax.ShapeDtypeStruct((n_out, d), src.dtype),
        grid_spec=pltpu.PrefetchScalarGridSpec(
            num_scalar_prefetch=1,
            grid=(n // ch,),
            in_specs=[smem_blk, smem_blk] + in_specs,
            out_specs=out_spec,
            scratch_shapes=[pltpu.SemaphoreType.DMA(())],
        ),
        input_output_aliases=aliases,
        compiler_params=_cparams(("arbitrary",)),
        name="moe_scatter" if scatter else "moe_gather",
    )(offs, bucket, rank, *args)


def _ffn_kernel(ea_ref, eb_ref, nv_ref, hs_ref, fn_ref, wr_ref, w1a, w3a, w2a, w1b, w3b, w2b, o_ref):
    i = pl.program_id(0)

    @pl.when(nv_ref[i] == 0)
    def _():
        o_ref[...] = hs_ref[...]

    @pl.when(nv_ref[i] > 0)
    def _():
        x = hs_ref[...]
        xn = _rms(x, fn_ref[...])
        logits, xb = _router_logits(xn, wr_ref)
        score = jax.nn.sigmoid(logits)
        lane = lax.broadcasted_iota(jnp.int32, score.shape, 1)
        sa = jnp.sum(jnp.where(lane == ea_ref[i], score, 0.0), axis=-1, keepdims=True)
        sb = jnp.sum(jnp.where(lane == eb_ref[i], score, 0.0), axis=-1, keepdims=True)
        tot = sa + sb

        def expert(w1, w3, w2, gate):
            h1 = jnp.dot(xb, w1[...], preferred_element_type=F32)
            h3 = jnp.dot(xb, w3[...], preferred_element_type=F32)
            act = (h1 * jax.nn.sigmoid(h1)) * h3 * gate
            return jnp.dot(act.astype(BF16), w2[...], preferred_element_type=F32)

        o_ref[...] = x + expert(w1a, w3a, w2a, sa / tot) + expert(w1b, w3b, w2b, sb / tot)


def _moe_ffn(hs, fn, wr, w1, w3, w2, tile_ea, tile_eb, tile_nv, *, tm):
    ns, d = hs.shape
    de = w1.shape[2]
    ntiles = ns // tm
    const = lambda i, *_: (0, 0)
    wa = lambda i, ea, eb, nv: (ea[i], 0, 0)
    wb = lambda i, ea, eb, nv: (eb[i], 0, 0)
    return pl.pallas_call(
        _ffn_kernel,
        out_shape=jax.ShapeDtypeStruct((ns, d), F32),
        grid_spec=pltpu.PrefetchScalarGridSpec(
            num_scalar_prefetch=3,
            grid=(ntiles,),
            in_specs=[
                pl.BlockSpec((tm, d), lambda i, *_: (i, 0)),
                pl.BlockSpec((1, d), const),
                pl.BlockSpec((d, 2 * LANES), const),
                pl.BlockSpec((None, d, de), wa),
                pl.BlockSpec((None, d, de), wa),
                pl.BlockSpec((None, de, d), wa),
                pl.BlockSpec((None, d, de), wb),
                pl.BlockSpec((None, d, de), wb),
                pl.BlockSpec((None, de, d), wb),
            ],
            out_specs=pl.BlockSpec((tm, d), lambda i, *_: (i, 0)),
        ),
        compiler_params=_cparams(("arbitrary",)),
        name="moe_ffn",
    )(tile_ea, tile_eb, tile_nv, hs, fn, wr, w1, w3, w2, w1, w3, w2)


def _router_weights(router_w):
    d = router_w.shape[0]
    hi, lo = _split_bf16(router_w)
    wr = jnp.zeros((d, 2 * LANES), BF16)
    return wr.at[:, :N_EXPERTS].set(hi).at[:, LANES:LANES + N_EXPERTS].set(lo)


def _moe_layer(h, fn, wr, bias, w1, w3, w2, *, tm=512):
    n, d = h.shape
    fn = fn.reshape(1, d)
    bucket, rank, counts = _moe_route(h, fn, wr, bias)
    counts = counts[:N_BUCKETS, 0].astype(jnp.int32)
    ntile_b = (counts + tm - 1) // tm
    tile_end = jnp.cumsum(ntile_b)
    tile_start = tile_end - ntile_b
    offs = jnp.zeros((BUCKET_ROWS,), jnp.int32).at[:N_BUCKETS].set(tile_start * tm)
    ntiles = n // tm + N_BUCKETS
    j = jnp.arange(ntiles, dtype=jnp.int32)
    used = j < tile_end[-1]
    tb = jnp.minimum(jnp.searchsorted(tile_end, j, side="right").astype(jnp.int32), N_BUCKETS - 1)
    last_b = jnp.max(jnp.where(ntile_b > 0, jnp.arange(N_BUCKETS, dtype=jnp.int32), 0))
    tb = jnp.where(used, tb, last_b)
    nv = jnp.where(used, jnp.clip(counts[tb] - (j - tile_start[tb]) * tm, 0, tm), 0).astype(jnp.int32)
    pair_lo = jnp.asarray([p[0] for p in PAIRS], jnp.int32)
    pair_hi = jnp.asarray([p[1] for p in PAIRS], jnp.int32)
    ea = GROUP_SIZE * (tb // len(PAIRS)) + pair_lo[tb % len(PAIRS)]
    eb = GROUP_SIZE * (tb // len(PAIRS)) + pair_hi[tb % len(PAIRS)]
    hs = _permute_rows(h, offs, bucket, rank, n_out=ntiles * tm, scatter=True)
    ys = _moe_ffn(hs, fn, wr, w1, w3, w2, ea, eb, nv, tm=tm)
    return _permute_rows(ys, offs, bucket, rank, n_out=n, scatter=False)


N_HEADS = 16
N_KV_GROUPS = 4
HEADS_PER_GROUP = N_HEADS // N_KV_GROUPS
HEAD_DIM = 64
CMP_LEN = 32
CMP_STRIDE = 16
SLC_LEN = 64
SLC_SHIFT = 6
SLC_TOP_N = 16
WINDOW = 512
N_BRANCH = 3
MASKED = -1e30
LOG2E = 1.4426950408889634
NT_DIMS = (((1,), (1,)), ((), ()))


def _segment_matrix():
    seg = np.kron(np.eye(LANES // HEAD_DIM), np.full((HEAD_DIM, HEAD_DIM), 1.0 / HEAD_DIM))
    return jnp.asarray(seg, BF16)


def _placement_matrix():
    p = np.zeros((LANES, 4 * LANES), np.float32)
    for half in range(2):
        for l in range(HEAD_DIM):
            p[half * HEAD_DIM + l, half * 2 * LANES + l] = 1.0
            p[half * HEAD_DIM + l, half * 2 * LANES + 3 * HEAD_DIM + l] = 1.0
    return jnp.asarray(p, BF16)


def _seg_rms(x, seg, gain):
    hi, lo = _split_bf16(x * x)
    ms = jnp.dot(hi, seg, preferred_element_type=F32) + jnp.dot(lo, seg, preferred_element_type=F32)
    return x * lax.rsqrt(ms + EPS) * gain


def _kv_proj_kernel(x_ref, kvn_ref, wkv_ref, seg_ref, pslab_ref, kgain_ref,
                    kcvc_ref, ks_ref, vs_ref, kw_ref, vw_ref):
    xn = _rms(x_ref[...], kvn_ref[...]).astype(BF16)
    kv = jnp.dot(xn, wkv_ref[...], preferred_element_type=F32)
    width = N_KV_GROUPS * HEAD_DIM
    for s in range(4):
        kcvc_ref[s] = kv[:, LANES * s:LANES * (s + 1)]
    seg = seg_ref[...]
    pslab = pslab_ref[...]

    def place(i, out_ref, gain_row):
        for s in range(width // LANES):
            slab = kv[:, width * i + LANES * s:width * i + LANES * (s + 1)]
            if gain_row is not None:
                slab = _seg_rms(slab, seg, kgain_ref[gain_row:gain_row + 1, :])
            out_ref[:, 4 * LANES * s:4 * LANES * (s + 1)] = jnp.dot(
                slab.astype(BF16), pslab, preferred_element_type=F32).astype(BF16)

    place(2, ks_ref, 1)
    place(3, vs_ref, None)
    place(4, kw_ref, 2)
    place(5, vw_ref, None)


def _kv_proj(x, kv_norm, w_kv, k_norm, *, t=512):
    n, d = x.shape
    width = N_KV_GROUPS * HEAD_DIM
    kgain = jnp.zeros((8, LANES), F32).at[:N_BRANCH].set(jnp.tile(k_norm, (1, LANES // HEAD_DIM)))
    const = lambda i: (0, 0)
    wide = jax.ShapeDtypeStruct((n, 4 * width), BF16)
    wide_spec = pl.BlockSpec((t, 4 * width), lambda i: (i, 0))
    return pl.pallas_call(
        _kv_proj_kernel,
        out_shape=(jax.ShapeDtypeStruct((4, n, LANES), F32), wide, wide, wide, wide),
        grid=(n // t,),
        in_specs=[
            pl.BlockSpec((t, d), lambda i: (i, 0)),
            pl.BlockSpec((1, d), const),
            pl.BlockSpec((d, 6 * width), const),
            pl.BlockSpec((LANES, LANES), const),
            pl.BlockSpec((LANES, 4 * LANES), const),
            pl.BlockSpec((8, LANES), const),
        ],
        out_specs=(pl.BlockSpec((4, t, LANES), lambda i: (0, i, 0)), wide_spec, wide_spec, wide_spec, wide_spec),
        compiler_params=_cparams(("arbitrary",)),
        name="nsa_kv_proj",
    )(x, kv_norm.reshape(1, d), w_kv.astype(BF16), _segment_matrix(), _placement_matrix(), kgain)


def _compress_kernel(x_ref, pe_ref, w1_ref, b1_ref, w2_ref, seg_ref, pslab_ref, gain_ref, o_ref, bm_scr,
                     *, normalize):
    seq = x_ref.shape[1]
    nc = seq // CMP_STRIDE
    halves = CMP_LEN // CMP_STRIDE
    assert halves == 2
    hidden = w1_ref.shape[2]
    for slab in range(2):
        first = jnp.zeros((nc, hidden), F32)
        second = jnp.zeros((nc, hidden), F32)
        for j in range(CMP_STRIDE):
            xj = x_ref[slab, pl.ds(j, nc, stride=CMP_STRIDE), :]
            first += jnp.dot((xj + pe_ref[j:j + 1, :]).astype(BF16), w1_ref[j], preferred_element_type=F32)
            second += jnp.dot((xj + pe_ref[CMP_STRIDE + j:CMP_STRIDE + j + 1, :]).astype(BF16),
                              w1_ref[CMP_STRIDE + j], preferred_element_type=F32)
        bm_scr[0:nc, :] = second
        bm_scr[nc:nc + 8, :] = jnp.zeros((8, hidden), F32)
        z = first + bm_scr[1:nc + 1, :] + b1_ref[...]
        hid = jax.nn.gelu(z)
        out = jnp.dot(hid.astype(BF16), w2_ref[...], preferred_element_type=F32)
        if normalize:
            out = _seg_rms(out, seg_ref[...], gain_ref[0:1, :])
        out = jnp.where(lax.broadcasted_iota(jnp.int32, out.shape, 0) < nc - 1, out, 0.0)
        o_ref[:, 4 * LANES * slab:4 * LANES * (slab + 1)] = jnp.dot(
            out.astype(BF16), pslab_ref[...], preferred_element_type=F32).astype(BF16)


def _compress(kcvc, which, pe, w1, b1, w2, gain, *, batch, normalize):
    n = kcvc.shape[1]
    seq = n // batch
    nc = seq // CMP_STRIDE
    hidden = w1.shape[1]
    eye2 = jnp.eye(2, dtype=F32)
    w1j = w1.reshape(CMP_LEN, HEAD_DIM, hidden)
    w1bd = jax.vmap(lambda m: jnp.kron(eye2, m))(w1j).astype(BF16)
    w2bd = jnp.kron(eye2, w2).astype(BF16)
    pe2 = jnp.tile(pe, (1, 2))
    b12 = jnp.tile(b1.reshape(1, hidden), (1, 2))
    gain2 = jnp.zeros((8, LANES), F32).at[0].set(jnp.tile(gain, 2))
    const2 = lambda b: (0, 0)
    return pl.pallas_call(
        functools.partial(_compress_kernel, normalize=normalize),
        out_shape=jax.ShapeDtypeStruct((batch, nc, 8 * LANES), BF16),
        grid=(batch,),
        in_specs=[
            pl.BlockSpec((2, seq, LANES), lambda b: (which, b, 0)),
            pl.BlockSpec((CMP_LEN, LANES), const2),
            pl.BlockSpec((CMP_LEN, LANES, 2 * hidden), lambda b: (0, 0, 0)),
            pl.BlockSpec((1, 2 * hidden), const2),
            pl.BlockSpec((2 * hidden, LANES), const2),
            pl.BlockSpec((LANES, LANES), const2),
            pl.BlockSpec((LANES, 4 * LANES), const2),
            pl.BlockSpec((8, LANES), const2),
        ],
        out_specs=pl.BlockSpec((None, nc, 8 * LANES), lambda b: (b, 0, 0)),
        scratch_shapes=[pltpu.VMEM((nc + 8, 2 * hidden), F32)],
        compiler_params=_cparams(("arbitrary",)),
        name="nsa_compress",
    )(kcvc, pe2, w1bd, b12, w2bd, _segment_matrix(), _placement_matrix(), gain2)


def _nsa_proj_kernel(h_ref, bn_ref, win_ref, seg_ref, qgain_ref, q_ref, g_ref):
    xn = _rms(h_ref[...], bn_ref[...]).astype(BF16)
    proj = jnp.dot(xn, win_ref[...], preferred_element_type=F32)
    nq = q_ref.shape[1]
    seg = seg_ref[...]
    for s in range(nq // LANES):
        slab = proj[:, LANES * s:LANES * (s + 1)]
        q_ref[:, LANES * s:LANES * (s + 1)] = _seg_rms(slab, seg, qgain_ref[...]).astype(BF16)
    g_ref[...] = jax.nn.sigmoid(proj[:, nq:nq + LANES])


def _nsa_proj(h, b_norm, w_in, q_norm, *, t=512):
    n, d = h.shape
    nq = N_HEADS * HEAD_DIM
    win = jnp.zeros((d, nq + LANES), BF16).at[:, :w_in.shape[1]].set(w_in.astype(BF16))
    qgain = jnp.tile(q_norm * (HEAD_DIM ** -0.5 * LOG2E), LANES // HEAD_DIM).reshape(1, LANES)
    const = lambda i: (0, 0)
    return pl.pallas_call(
        _nsa_proj_kernel,
        out_shape=(jax.ShapeDtypeStruct((n, nq), BF16), jax.ShapeDtypeStruct((n, LANES), F32)),
        grid=(n // t,),
        in_specs=[
            pl.BlockSpec((t, d), lambda i: (i, 0)),
            pl.BlockSpec((1, d), const),
            pl.BlockSpec((d, nq + LANES), const),
            pl.BlockSpec((LANES, LANES), const),
            pl.BlockSpec((1, LANES), const),
        ],
        out_specs=(pl.BlockSpec((t, nq), lambda i: (i, 0)), pl.BlockSpec((t, LANES), lambda i: (i, 0))),
        compiler_params=_cparams(("arbitrary",)),
        name="nsa_proj",
    )(h, b_norm.reshape(1, d), win, _segment_matrix(), qgain)


def _nsa_attn_kernel(cmask_ref, q_ref, g_ref, kc_ref, vc_ref, ks_ref, vs_ref, kw_ref, vw_ref, mapm_ref, gsel_ref,
                     o_ref, m_scr, acc_scr, *, tq):
    step = pl.program_id(2)
    q0 = step * tq
    seq = ks_ref.shape[0]
    rows = 2 * tq
    qq = jnp.concatenate([q_ref[:, 0:LANES], q_ref[:, LANES:2 * LANES]], axis=0)
    t = q0 + lax.broadcasted_iota(jnp.int32, (tq, 1), 0)
    t2 = jnp.concatenate([t, t], axis=0)
    lane_t = lax.broadcasted_iota(jnp.int32, (tq, LANES), 1)
    tk = 2 * tq
    lane_k = lax.broadcasted_iota(jnp.int32, (tk, LANES), 1)
    key_row = lax.broadcasted_iota(jnp.int32, (tk, LANES), 0)
    ones_half = (jnp.where(lane_k < HEAD_DIM, 1.0, 0.0).astype(BF16),
                 jnp.where(lane_k >= HEAD_DIM, 1.0, 0.0).astype(BF16))

    def reset():
        m_scr[...] = jnp.full(m_scr.shape, MASKED, F32)
        acc_scr[...] = jnp.zeros(acc_scr.shape, F32)

    def attend(lhs, k_blk, v_blk, valid, k_extra=None):
        for half in range(2):
            rhs = k_blk[:, LANES * half:LANES * (half + 1)]
            if k_extra is not None:
                rhs = jnp.concatenate([rhs, k_extra], axis=1)
            v_aug = jnp.concatenate([v_blk[:, LANES * half:LANES * (half + 1)], ones_half[half]], axis=1)
            for pair in range(2):
                r = slice(pair * tq, (pair + 1) * tq)
                s = lax.dot_general(lhs[r], rhs, NT_DIMS, preferred_element_type=F32)
                if valid is not None:
                    s = jnp.where(valid, s, MASKED)
                m_old = m_scr[half, r, :]
                m_new = jnp.maximum(m_old, jnp.max(s, axis=-1, keepdims=True))
                m_scr[half, r, :] = m_new
                p = jnp.exp2(s - m_new[:, 0:1]).astype(BF16)
                alpha = jnp.exp2(m_old - m_new)
                acc_scr[half, r, :] = (acc_scr[half, r, :] * jnp.concatenate([alpha, alpha], axis=1)
                                       + jnp.dot(p, v_aug, preferred_element_type=F32))

    def result():
        acc = acc_scr[0] + acc_scr[1]
        return acc[:, :LANES] / acc[:, LANES:]

    kc = kc_ref[...]
    vc = vc_ref[...]
    nc = kc.shape[0]
    cmp_end = lax.broadcasted_iota(jnp.int32, (1, nc), 1) * CMP_STRIDE + (CMP_LEN - 1)
    valid_c = cmp_end <= t2
    row_ok = t2 >= CMP_LEN - 1

    def cmp_probs(half):
        s = lax.dot_general(qq, kc[:, LANES * half:LANES * (half + 1)], NT_DIMS, preferred_element_type=F32)
        s = jnp.where(valid_c, s, MASKED)
        e = jnp.exp2(s - jnp.max(s, axis=-1, keepdims=True))
        return e * jnp.where(row_ok, 1.0 / jnp.sum(e, axis=-1, keepdims=True), 0.0)

    p_e = cmp_probs(0)
    p_o = cmp_probs(1)
    o_cmp = (jnp.dot(p_e.astype(BF16), vc[:, :LANES], preferred_element_type=F32)
             + jnp.dot(p_o.astype(BF16), vc[:, LANES:], preferred_element_type=F32))
    p_sum = (p_e[:tq] + p_o[:tq]) + (p_e[tq:] + p_o[tq:])
    hi, lo = _split_bf16(p_sum)
    mapm = mapm_ref[...]
    imp = jnp.dot(hi, mapm, preferred_element_type=F32) + jnp.dot(lo, mapm, preferred_element_type=F32)

    blk = lax.broadcasted_iota(jnp.int32, (LANES, tq), 0)
    blk_f = blk.astype(F32)
    cur = jnp.right_shift(q0 + lax.broadcasted_iota(jnp.int32, (1, tq), 1), SLC_SHIFT)
    forced = (blk == 0) | (blk == cur) | (blk == cur - 1)
    work = jnp.where(forced, jnp.inf, imp.T)
    work = jnp.where(blk > cur, NEG_INF, work)
    sel_t = jnp.zeros((LANES, tq), F32)
    for _ in range(SLC_TOP_N):
        m = jnp.max(work, axis=0, keepdims=True)
        idx = jnp.min(jnp.where(work == m, blk_f, float(LANES)), axis=0, keepdims=True)
        pick = blk_f == idx
        sel_t = jnp.where(pick, 1.0, sel_t)
        work = jnp.where(pick, NEG_INF, work)
    sel = sel_t.T

    unsel = jnp.where(sel > 0.5, 0.0, cmask_ref[0]).astype(BF16)
    lhs = jnp.concatenate([qq, jnp.concatenate([unsel, unsel], axis=0)], axis=1)
    reset()

    def key_tile(j, causal):
        k0 = pl.multiple_of(j * tk, tk)
        key_block = jnp.where(jnp.right_shift(k0 + key_row, SLC_SHIFT) == lane_k, -1.0, 0.0).astype(BF16)
        valid = (k0 + lax.broadcasted_iota(jnp.int32, (1, tk), 1)) <= t if causal else None
        attend(lhs, ks_ref[pl.ds(k0, tk), :], vs_ref[pl.ds(k0, tk), :], valid, k_extra=key_block)

    def full_tile(j, carry):
        key_tile(j, False)
        return carry

    last = (step + 2) // 2 - 1
    lax.fori_loop(0, last, full_tile, 0)
    key_tile(last, True)
    o_sel = result()

    reset()
    w0 = pl.multiple_of(jnp.clip(q0 - WINDOW, 0, seq - tk), tq)
    dist = t - (w0 + lax.broadcasted_iota(jnp.int32, (1, tk), 1))
    attend(qq, kw_ref[pl.ds(w0, tk), :], vw_ref[pl.ds(w0, tk), :], (dist >= 0) & (dist < WINDOW))
    o_win = result()

    gh, gl = _split_bf16(g_ref[...])
    gsel = gsel_ref[...]
    gx = jnp.dot(gh, gsel, preferred_element_type=F32) + jnp.dot(gl, gsel, preferred_element_type=F32)
    out = jnp.zeros((rows, LANES), F32)
    for br, o_br in enumerate((o_cmp, o_sel, o_win)):
        gate = jnp.concatenate([gx[:, LANES * br:LANES * (br + 1)],
                                gx[:, LANES * (N_BRANCH + br):LANES * (N_BRANCH + br + 1)]], axis=0)
        out = out + gate * o_br
    o_ref[:, 0:LANES] = out[:tq].astype(o_ref.dtype)
    o_ref[:, LANES:2 * LANES] = out[tq:].astype(o_ref.dtype)


def _importance_map(nc, n_cmp):
    ratio, span = SLC_LEN // CMP_STRIDE, CMP_LEN // CMP_STRIDE
    w = np.convolve(np.ones(ratio), np.ones(span))
    m = np.zeros((nc, LANES), np.float32)
    for j in range(LANES):
        for o, wt in enumerate(w):
            c = ratio * j + o - (span - 1)
            if 0 <= c < n_cmp:
                m[c, j] = wt
    return jnp.asarray(m, BF16)


def _gate_select():
    m = np.zeros((N_KV_GROUPS, LANES, 2 * N_BRANCH * LANES), np.float32)
    for g in range(N_KV_GROUPS):
        for pair in range(2):
            for br in range(N_BRANCH):
                for lane in range(LANES):
                    r = 2 * pair + lane // HEAD_DIM
                    m[g, (g * HEADS_PER_GROUP + r) * N_BRANCH + br, (pair * N_BRANCH + br) * LANES + lane] = 1.0
    return jnp.asarray(m, BF16)


def _nsa_attention(q, gates, shared, q_norm, k_norm, *, batch, tq=512):
    kcmp, vcmp, ks, vs, kw, vw = shared
    n = q.shape[0]
    seq = n // batch
    assert WINDOW == tq and seq % (2 * tq) == 0
    nq = seq // tq
    nc = kcmp.shape[1]
    width = 2 * LANES
    ks, vs, kw, vw = (a.reshape(batch, seq, N_KV_GROUPS * width) for a in (ks, vs, kw, vw))
    bound = 1.02 * LOG2E * HEAD_DIM ** 0.5 * jnp.max(jnp.abs(q_norm)) * jnp.max(jnp.abs(k_norm[1]))
    cmask = jnp.exp2(jnp.ceil(jnp.log2(2.0 * bound + 1000.0))).reshape(1).astype(F32)
    qrow = lambda b, g, i, *_: (b * nq + i, g)
    per_bg = lambda b, g, i, *_: (b, 0, g)
    return pl.pallas_call(
        functools.partial(_nsa_attn_kernel, tq=tq),
        out_shape=jax.ShapeDtypeStruct((n, N_HEADS * HEAD_DIM), BF16),
        grid_spec=pltpu.PrefetchScalarGridSpec(
            num_scalar_prefetch=1,
            grid=(batch, N_KV_GROUPS, nq),
            in_specs=[
                pl.BlockSpec((tq, width), qrow),
                pl.BlockSpec((tq, LANES), lambda b, g, i, *_: (b * nq + i, 0)),
                pl.BlockSpec((None, nc, width), per_bg),
                pl.BlockSpec((None, nc, width), per_bg),
                pl.BlockSpec((None, seq, width), per_bg),
                pl.BlockSpec((None, seq, width), per_bg),
                pl.BlockSpec((None, seq, width), per_bg),
                pl.BlockSpec((None, seq, width), per_bg),
                pl.BlockSpec((nc, LANES), lambda b, g, i, *_: (0, 0)),
                pl.BlockSpec((None, LANES, 2 * N_BRANCH * LANES), lambda b, g, i, *_: (g, 0, 0)),
            ],
            out_specs=pl.BlockSpec((tq, width), qrow),
            scratch_shapes=[pltpu.VMEM((2, 2 * tq, LANES), F32), pltpu.VMEM((2, 2 * tq, 2 * LANES), F32)],
        ),
        compiler_params=_cparams(("arbitrary", "arbitrary", "arbitrary")),
        name="nsa_attention",
    )(cmask, q, gates, kcmp, vcmp, ks, vs, kw, vw,
      _importance_map(nc, (seq - CMP_LEN) // CMP_STRIDE + 1), _gate_select())


def _out_proj_kernel(h_ref, o_ref, w_ref, out_ref):
    out_ref[...] = h_ref[...] + jnp.dot(o_ref[...], w_ref[...], preferred_element_type=F32)


def _out_proj(h, o, w_out, *, t=512):
    n, d = h.shape
    k = o.shape[1]
    return pl.pallas_call(
        _out_proj_kernel,
        out_shape=jax.ShapeDtypeStruct((n, d), F32),
        grid=(n // t,),
        in_specs=[pl.BlockSpec((t, d), lambda i: (i, 0)), pl.BlockSpec((t, k), lambda i: (i, 0)),
                  pl.BlockSpec((k, d), lambda i: (0, 0))],
        out_specs=pl.BlockSpec((t, d), lambda i: (i, 0)),
        compiler_params=_cparams(("arbitrary",)),
        name="nsa_out_proj",
    )(h, o, w_out.astype(BF16))


def _nsa_shared_kv(x, kv_norm, w_kv, cmp_pe, phi_w1, phi_b1, phi_w2, k_norm, *, batch):
    kcvc, ks, vs, kw, vw = _kv_proj(x, kv_norm, w_kv, k_norm)
    kcmp = _compress(kcvc, 0, cmp_pe[0], phi_w1[0], phi_b1[0], phi_w2[0], k_norm[0], batch=batch, normalize=True)
    vcmp = _compress(kcvc, 1, cmp_pe[1], phi_w1[1], phi_b1[1], phi_w2[1], k_norm[0], batch=batch, normalize=False)
    return kcmp, vcmp, ks, vs, kw, vw


def _nsa_layer(h, b_norm, w_in, q_norm, w_out, k_norm, shared, *, batch):
    q, gates = _nsa_proj(h, b_norm, w_in, q_norm)
    o = _nsa_attention(q, gates, shared, q_norm, k_norm, batch=batch)
    return _out_proj(h, o, w_out)


def kernel(x, a_norm, a_w_in, a_dw, a_dw_bias, a_conv_norm, a_w_out, kv_norm, w_kv, cmp_pe, phi_w1, phi_b1, phi_w2, k_norm, b_norm, b_w_in, b_q_norm, b_w_out, ffn_norm, router_w, router_bias, moe_w1, moe_w3, moe_w2):
    bsz, s, d = x.shape
    h = x.reshape(bsz * s, d)
    wr = _router_weights(router_w)
    n_a = a_norm.shape[0]
    depth = ffn_norm.shape[0]
    shared = None
    for layer in range(depth):
        if layer < n_a:
            i = layer
            h = _conv_layer(h, a_norm[i], a_w_in[i], a_dw[i], a_dw_bias[i], a_conv_norm[i], a_w_out[i], batch=bsz)
        else:
            i = layer - n_a
            h = _nsa_layer(h, b_norm[i], b_w_in[i], b_q_norm[i], b_w_out[i], k_norm, shared, batch=bsz)
        h = _moe_layer(h, ffn_norm[layer], wr, router_bias, moe_w1[layer].astype(BF16),
                       moe_w3[layer].astype(BF16), moe_w2[layer].astype(BF16))
        if layer == n_a - 1:
            shared = _nsa_shared_kv(h, kv_norm, w_kv, cmp_pe, phi_w1, phi_b1, phi_w2, k_norm, batch=bsz)
    return h.reshape(bsz, s, d)
```

```python
import functools

import jax
import jax.numpy as jnp
import numpy as np
from jax import lax
from jax.experimental import pallas as pl
from jax.experimental.pallas import tpu as pltpu

F32 = jnp.float32
BF16 = jnp.bfloat16

EPS = 1e-6
CONV_WIDTH = 31
CONV_PAD = 32
VMEM_LIMIT = 56 * 1024 * 1024


def _rms(x, g):
    ms = jnp.mean(x * x, axis=-1, keepdims=True)
    return x * lax.rsqrt(ms + EPS) * g


def _cparams(sem):
    return pltpu.CompilerParams(dimension_semantics=sem, vmem_limit_bytes=VMEM_LIMIT)


CONV_ROWS = 64
CONV_LANES = 256


def _conv_layer_kernel(h_ref, an_ref, win_ref, dw_ref, dwb_ref, cn_ref, wout_ref, o_ref, ubuf, ybuf):
    ts, c = ybuf.shape
    t = pl.program_id(1)

    @pl.when(t == 0)
    def _():
        ubuf[:, 0:CONV_PAD, :] = jnp.zeros((8, CONV_PAD, c), F32)

    @pl.when(t != 0)
    def _():
        ubuf[:, 0:CONV_PAD, :] = ubuf[:, ts:ts + CONV_PAD, :]

    x = h_ref[...]
    xn = _rms(x, an_ref[...]).astype(BF16)
    ab = jnp.dot(xn, win_ref[...], preferred_element_type=F32)
    u = ab[:, :c] * jax.nn.sigmoid(ab[:, c:])
    for rho in range(8):
        ubuf[rho, CONV_PAD - rho:CONV_PAD - rho + ts, :] = u

    base_off = CONV_PAD - (CONV_WIDTH - 1)

    def chunk(rc, carry):
        r0 = pl.multiple_of(rc * CONV_ROWS, CONV_ROWS)
        for lc in range(c // CONV_LANES):
            lanes = slice(lc * CONV_LANES, (lc + 1) * CONV_LANES)
            acc = jnp.broadcast_to(dwb_ref[:, lanes], (CONV_ROWS, CONV_LANES))
            for rho in range(8):
                taps = [k for k in range(CONV_WIDTH) if (base_off + k) % 8 == rho]
                first = base_off + taps[0] - rho
                nrows = CONV_ROWS + (taps[-1] - taps[0])
                blk = ubuf[rho, pl.ds(r0 + first, nrows), lanes]
                for k in taps:
                    a = k - taps[0]
                    acc = acc + dw_ref[k:k + 1, lanes] * blk[a:a + CONV_ROWS, :]
            ybuf[pl.ds(r0, CONV_ROWS), lanes] = acc
        return carry

    lax.fori_loop(0, ts // CONV_ROWS, chunk, 0)

    y = ybuf[...]
    yn = _rms(y, cn_ref[...])
    act = (yn * jax.nn.sigmoid(yn)).astype(BF16)
    o_ref[...] = x + jnp.dot(act, wout_ref[...], preferred_element_type=F32)


def _conv_layer(h, a_norm, w_in, dw, dw_bias, conv_norm, w_out, *, batch, ts=256):
    n, d = h.shape
    c = w_out.shape[0]
    seq = n // batch
    nts = seq // ts
    dwp = jnp.zeros((CONV_PAD, c), F32).at[:CONV_WIDTH].set(dw)
    row = lambda b, t: (b * nts + t, 0)
    const = lambda b, t: (0, 0)
    return pl.pallas_call(
        _conv_layer_kernel,
        out_shape=jax.ShapeDtypeStruct((n, d), F32),
        grid=(batch, nts),
        in_specs=[
            pl.BlockSpec((ts, d), row),
            pl.BlockSpec((1, d), const),
            pl.BlockSpec((d, 2 * c), const),
            pl.BlockSpec((CONV_PAD, c), const),
            pl.BlockSpec((1, c), const),
            pl.BlockSpec((1, c), const),
            pl.BlockSpec((c, d), const),
        ],
        out_specs=pl.BlockSpec((ts, d), row),
        scratch_shapes=[pltpu.VMEM((8, ts + CONV_PAD, c), F32), pltpu.VMEM((ts, c), F32)],
        compiler_params=_cparams(("arbitrary", "arbitrary")),
        name="conv_layer",
    )(h, a_norm.reshape(1, d), w_in.astype(BF16), dwp, dw_bias.reshape(1, c),
      conv_norm.reshape(1, c), w_out.astype(BF16))


N_EXPERTS = 16
N_GROUPS = 4
GROUP_SIZE = N_EXPERTS // N_GROUPS
PAIRS = ((0, 1), (0, 2), (0, 3), (1, 2), (1, 3), (2, 3))
N_BUCKETS = N_GROUPS * len(PAIRS)
BUCKET_ROWS = 32
LANES = 128
NEG_INF = float("-inf")


def _split_bf16(x):
    hi = x.astype(BF16)
    lo = (x - hi.astype(F32)).astype(BF16)
    return hi, lo


def _router_logits(xn, wr_ref):
    hi, lo = _split_bf16(xn)
    w = wr_ref[...]
    both = jnp.dot(hi, w, preferred_element_type=F32)
    low = jnp.dot(lo, w[:, :LANES], preferred_element_type=F32)
    return both[:, :LANES] + both[:, LANES:] + low, hi


def _route_kernel(bias_ref, h_ref, fn_ref, wr_ref, tri_ref, bucket_ref, rank_ref, counts_ref, carry):
    tr = h_ref.shape[0]

    @pl.when(pl.program_id(0) == 0)
    def _():
        carry[...] = jnp.zeros_like(carry)

    xn = _rms(h_ref[...], fn_ref[...])
    logits, _ = _router_logits(xn, wr_ref)
    lt = logits.T
    score = jax.nn.sigmoid(lt[0:N_EXPERTS, :])
    biased = [score[e:e + 1, :] + bias_ref[e] for e in range(N_EXPERTS)]

    best = None
    for g in range(N_GROUPS):
        v = biased[GROUP_SIZE * g:GROUP_SIZE * (g + 1)]
        gs = None
        for a, b in PAIRS:
            gs = v[a] + v[b] if gs is None else jnp.maximum(gs, v[a] + v[b])
        if best is None:
            best, gsel = gs, jnp.zeros_like(gs, dtype=jnp.int32)
        else:
            better = gs > best
            gsel = jnp.where(better, g, gsel)
            best = jnp.where(better, gs, best)

    vb = []
    for i in range(GROUP_SIZE):
        val = biased[i]
        for g in range(1, N_GROUPS):
            val = jnp.where(gsel == g, biased[GROUP_SIZE * g + i], val)
        vb.append(val)

    def first_argmax(vals):
        m, idx = vals[0], jnp.zeros_like(gsel)
        for i in range(1, GROUP_SIZE):
            better = vals[i] > m
            idx = jnp.where(better, i, idx)
            m = jnp.where(better, vals[i], m)
        return idx

    i1 = first_argmax(vb)
    i2 = first_argmax([jnp.where(i1 == i, NEG_INF, vb[i]) for i in range(GROUP_SIZE)])
    lo = jnp.minimum(i1, i2)
    hi = jnp.maximum(i1, i2)
    pair = jnp.where(lo == 0, 0, jnp.where(lo == 1, 3, 5)) + hi - lo - 1
    bucket = gsel * len(PAIRS) + pair

    onehot = lax.broadcasted_iota(jnp.int32, (BUCKET_ROWS, tr), 0) == bucket
    before = jnp.dot(onehot.astype(BF16), tri_ref[...], preferred_element_type=F32)
    onef = onehot.astype(F32)
    rank = jnp.sum(onef * (before + carry[:, 0:1]), axis=0, keepdims=True)
    carry[...] = carry[...] + jnp.sum(onef, axis=1, keepdims=True)
    bucket_ref[0] = bucket
    rank_ref[0] = rank.astype(jnp.int32)
    counts_ref[...] = carry[...]


def _moe_route(h, fn, wr, bias, *, tr=512):
    n, d = h.shape
    nt = n // tr
    tri = jnp.triu(jnp.ones((tr, tr), F32), 1).astype(BF16)
    const = lambda i, *_: (0, 0)
    bucket, rank, counts = pl.pallas_call(
        _route_kernel,
        out_shape=(jax.ShapeDtypeStruct((nt, 1, tr), jnp.int32),
                   jax.ShapeDtypeStruct((nt, 1, tr), jnp.int32),
                   jax.ShapeDtypeStruct((BUCKET_ROWS, LANES), F32)),
        grid_spec=pltpu.PrefetchScalarGridSpec(
            num_scalar_prefetch=1,
            grid=(nt,),
            in_specs=[
                pl.BlockSpec((tr, d), lambda i, *_: (i, 0)),
                pl.BlockSpec((1, d), const),
                pl.BlockSpec((d, 2 * LANES), const),
                pl.BlockSpec((tr, tr), const),
            ],
            out_specs=[
                pl.BlockSpec((1, 1, tr), lambda i, *_: (i, 0, 0)),
                pl.BlockSpec((1, 1, tr), lambda i, *_: (i, 0, 0)),
                pl.BlockSpec((BUCKET_ROWS, LANES), const),
            ],
            scratch_shapes=[pltpu.VMEM((BUCKET_ROWS, LANES), F32)],
        ),
        compiler_params=_cparams(("arbitrary",)),
        name="moe_route",
    )(bias, h, fn, wr, tri)
    return bucket, rank, counts


def _scatter_kernel(offs_ref, bucket_ref, rank_ref, src_ref, init_ref, dst_ref, sem, *, ch):
    del init_ref
    for j in range(ch):
        row = offs_ref[bucket_ref[0, 0, j]] + rank_ref[0, 0, j]
        pltpu.make_async_copy(src_ref.at[pl.ds(j, 1)], dst_ref.at[pl.ds(row, 1)], sem).start(priority=j % 2)
    pltpu.make_async_copy(src_ref, dst_ref.at[pl.ds(0, ch)], sem).wait()


def _gather_kernel(offs_ref, bucket_ref, rank_ref, src_ref, dst_ref, sem, *, ch):
    for j in range(ch):
        row = offs_ref[bucket_ref[0, 0, j]] + rank_ref[0, 0, j]
        pltpu.make_async_copy(src_ref.at[pl.ds(row, 1)], dst_ref.at[pl.ds(j, 1)], sem).start(priority=j % 2)
    pltpu.make_async_copy(src_ref.at[pl.ds(0, ch)], dst_ref, sem).wait()


def _permute_rows(src, offs, bucket, rank, *, n_out, scatter, ch=256):
    n = bucket.size
    d = src.shape[1]
    bucket = bucket.reshape(n // ch, 1, ch)
    rank = rank.reshape(n // ch, 1, ch)
    smem_blk = pl.BlockSpec((1, 1, ch), lambda i, *_: (i, 0, 0), memory_space=pltpu.SMEM)
    any_spec = pl.BlockSpec(memory_space=pl.ANY)
    tile_spec = pl.BlockSpec((ch, d), lambda i, *_: (i, 0))
    if scatter:
        body, args, in_specs, out_spec, aliases = (
            _scatter_kernel, [src, jnp.zeros((n_out, d), src.dtype)], [tile_spec, any_spec], any_spec, {4: 0})
    else:
        body, args, in_specs, out_spec, aliases = _gather_kernel, [src], [any_spec], tile_spec, {}
    return pl.pallas_call(
        functools.partial(body, ch=ch),
        out_shape=jax.ShapeDtypeStruct((n_out, d), src.dtype),
        grid_spec=pltpu.PrefetchScalarGridSpec(
            num_scalar_prefetch=1,
            grid=(n // ch,),
            in_specs=[smem_blk, smem_blk] + in_specs,
            out_specs=out_spec,
            scratch_shapes=[pltpu.SemaphoreType.DMA(())],
        ),
        input_output_aliases=aliases,
        compiler_params=_cparams(("arbitrary",)),
        name="moe_scatter" if scatter else "moe_gather",
    )(offs, bucket, rank, *args)


def _ffn_kernel(ea_ref, eb_ref, nv_ref, hs_ref, fn_ref, wr_ref, w1a, w3a, w2a, w1b, w3b, w2b, o_ref):
    i = pl.program_id(0)

    @pl.when(nv_ref[i] == 0)
    def _():
        o_ref[...] = hs_ref[...]

    @pl.when(nv_ref[i] > 0)
    def _():
        x = hs_ref[...]
        xn = _rms(x, fn_ref[...])
        logits, xb = _router_logits(xn, wr_ref)
        score = jax.nn.sigmoid(logits)
        lane = lax.broadcasted_iota(jnp.int32, score.shape, 1)
        sa = jnp.sum(jnp.where(lane == ea_ref[i], score, 0.0), axis=-1, keepdims=True)
        sb = jnp.sum(jnp.where(lane == eb_ref[i], score, 0.0), axis=-1, keepdims=True)
        tot = sa + sb

        def expert(w1, w3, w2, gate):
            h1 = jnp.dot(xb, w1[...], preferred_element_type=F32)
            h3 = jnp.dot(xb, w3[...], preferred_element_type=F32)
            act = (h1 * jax.nn.sigmoid(h1)) * h3 * gate
            return jnp.dot(act.astype(BF16), w2[...], preferred_element_type=F32)

        o_ref[...] = x + expert(w1a, w3a, w2a, sa / tot) + expert(w1b, w3b, w2b, sb / tot)


def _moe_ffn(hs, fn, wr, w1, w3, w2, tile_ea, tile_eb, tile_nv, *, tm):
    ns, d = hs.shape
    de = w1.shape[2]
    ntiles = ns // tm
    const = lambda i, *_: (0, 0)
    wa = lambda i, ea, eb, nv: (ea[i], 0, 0)
    wb = lambda i, ea, eb, nv: (eb[i], 0, 0)
    return pl.pallas_call(
        _ffn_kernel,
        out_shape=jax.ShapeDtypeStruct((ns, d), F32),
        grid_spec=pltpu.PrefetchScalarGridSpec(
            num_scalar_prefetch=3,
            grid=(ntiles,),
            in_specs=[
                pl.BlockSpec((tm, d), lambda i, *_: (i, 0)),
                pl.BlockSpec((1, d), const),
                pl.BlockSpec((d, 2 * LANES), const),
                pl.BlockSpec((None, d, de), wa),
                pl.BlockSpec((None, d, de), wa),
                pl.BlockSpec((None, de, d), wa),
                pl.BlockSpec((None, d, de), wb),
                pl.BlockSpec((None, d, de), wb),
                pl.BlockSpec((None, de, d), wb),
            ],
            out_specs=pl.BlockSpec((tm, d), lambda i, *_: (i, 0)),
        ),
        compiler_params=_cparams(("arbitrary",)),
        name="moe_ffn",
    )(tile_ea, tile_eb, tile_nv, hs, fn, wr, w1, w3, w2, w1, w3, w2)


def _router_weights(router_w):
    d = router_w.shape[0]
    hi, lo = _split_bf16(router_w)
    wr = jnp.zeros((d, 2 * LANES), BF16)
    return wr.at[:, :N_EXPERTS].set(hi).at[:, LANES:LANES + N_EXPERTS].set(lo)


def _moe_layer(h, fn, wr, bias, w1, w3, w2, *, tm=512):
    n, d = h.shape
    fn = fn.reshape(1, d)
    bucket, rank, counts = _moe_route(h, fn, wr, bias)
    counts = counts[:N_BUCKETS, 0].astype(jnp.int32)
    ntile_b = (counts + tm - 1) // tm
    tile_end = jnp.cumsum(ntile_b)
    tile_start = tile_end - ntile_b
    offs = jnp.zeros((BUCKET_ROWS,), jnp.int32).at[:N_BUCKETS].set(tile_start * tm)
    ntiles = n // tm + N_BUCKETS
    j = jnp.arange(ntiles, dtype=jnp.int32)
    used = j < tile_end[-1]
    tb = jnp.minimum(jnp.searchsorted(tile_end, j, side="right").astype(jnp.int32), N_BUCKETS - 1)
    last_b = jnp.max(jnp.where(ntile_b > 0, jnp.arange(N_BUCKETS, dtype=jnp.int32), 0))
    tb = jnp.where(used, tb, last_b)
    nv = jnp.where(used, jnp.clip(counts[tb] - (j - tile_start[tb]) * tm, 0, tm), 0).astype(jnp.int32)
    pair_lo = jnp.asarray([p[0] for p in PAIRS], jnp.int32)
    pair_hi = jnp.asarray([p[1] for p in PAIRS], jnp.int32)
    ea = GROUP_SIZE * (tb // len(PAIRS)) + pair_lo[tb % len(PAIRS)]
    eb = GROUP_SIZE * (tb // len(PAIRS)) + pair_hi[tb % len(PAIRS)]
    hs = _permute_rows(h, offs, bucket, rank, n_out=ntiles * tm, scatter=True)
    ys = _moe_ffn(hs, fn, wr, w1, w3, w2, ea, eb, nv, tm=tm)
    return _permute_rows(ys, offs, bucket, rank, n_out=n, scatter=False)


N_HEADS = 16
N_KV_GROUPS = 4
HEADS_PER_GROUP = N_HEADS // N_KV_GROUPS
HEAD_DIM = 64
CMP_LEN = 32
CMP_STRIDE = 16
SLC_LEN = 64
SLC_SHIFT = 6
SLC_TOP_N = 16
WINDOW = 512
N_BRANCH = 3
MASKED = -1e30
LOG2E = 1.4426950408889634
NT_DIMS = (((1,), (1,)), ((), ()))


def _segment_matrix():
    seg = np.kron(np.eye(LANES // HEAD_DIM), np.full((HEAD_DIM, HEAD_DIM), 1.0 / HEAD_DIM))
    return jnp.asarray(seg, BF16)


def _placement_matrix():
    p = np.zeros((LANES, 4 * LANES), np.float32)
    for half in range(2):
        for l in range(HEAD_DIM):
            p[half * HEAD_DIM + l, half * 2 * LANES + l] = 1.0
            p[half * HEAD_DIM + l, half * 2 * LANES + 3 * HEAD_DIM + l] = 1.0
    return jnp.asarray(p, BF16)


def _seg_rms(x, seg, gain):
    hi, lo = _split_bf16(x * x)
    ms = jnp.dot(hi, seg, preferred_element_type=F32) + jnp.dot(lo, seg, preferred_element_type=F32)
    return x * lax.rsqrt(ms + EPS) * gain


def _kv_proj_kernel(x_ref, kvn_ref, wkv_ref, seg_ref, pslab_ref, kgain_ref,
                    kcvc_ref, ks_ref, vs_ref, kw_ref, vw_ref):
    xn = _rms(x_ref[...], kvn_ref[...]).astype(BF16)
    kv = jnp.dot(xn, wkv_ref[...], preferred_element_type=F32)
    width = N_KV_GROUPS * HEAD_DIM
    for s in range(4):
        kcvc_ref[s] = kv[:, LANES * s:LANES * (s + 1)]
    seg = seg_ref[...]
    pslab = pslab_ref[...]

    def place(i, out_ref, gain_row):
        for s in range(width // LANES):
            slab = kv[:, width * i + LANES * s:width * i + LANES * (s + 1)]
            if gain_row is not None:
                slab = _seg_rms(slab, seg, kgain_ref[gain_row:gain_row + 1, :])
            out_ref[:, 4 * LANES * s:4 * LANES * (s + 1)] = jnp.dot(
                slab.astype(BF16), pslab, preferred_element_type=F32).astype(BF16)

    place(2, ks_ref, 1)
    place(3, vs_ref, None)
    place(4, kw_ref, 2)
    place(5, vw_ref, None)


def _kv_proj(x, kv_norm, w_kv, k_norm, *, t=512):
    n, d = x.shape
    width = N_KV_GROUPS * HEAD_DIM
    kgain = jnp.zeros((8, LANES), F32).at[:N_BRANCH].set(jnp.tile(k_norm, (1, LANES // HEAD_DIM)))
    const = lambda i: (0, 0)
    wide = jax.ShapeDtypeStruct((n, 4 * width), BF16)
    wide_spec = pl.BlockSpec((t, 4 * width), lambda i: (i, 0))
    return pl.pallas_call(
        _kv_proj_kernel,
        out_shape=(jax.ShapeDtypeStruct((4, n, LANES), F32), wide, wide, wide, wide),
        grid=(n // t,),
        in_specs=[
            pl.BlockSpec((t, d), lambda i: (i, 0)),
            pl.BlockSpec((1, d), const),
            pl.BlockSpec((d, 6 * width), const),
            pl.BlockSpec((LANES, LANES), const),
            pl.BlockSpec((LANES, 4 * LANES), const),
            pl.BlockSpec((8, LANES), const),
        ],
        out_specs=(pl.BlockSpec((4, t, LANES), lambda i: (0, i, 0)), wide_spec, wide_spec, wide_spec, wide_spec),
        compiler_params=_cparams(("arbitrary",)),
        name="nsa_kv_proj",
    )(x, kv_norm.reshape(1, d), w_kv.astype(BF16), _segment_matrix(), _placement_matrix(), kgain)


def _compress_kernel(x_ref, pe_ref, w1_ref, b1_ref, w2_ref, seg_ref, pslab_ref, gain_ref, o_ref, bm_scr,
                     *, normalize):
    seq = x_ref.shape[1]
    nc = seq // CMP_STRIDE
    halves = CMP_LEN // CMP_STRIDE
    assert halves == 2
    hidden = w1_ref.shape[2]
    for slab in range(2):
        first = jnp.zeros((nc, hidden), F32)
        second = jnp.zeros((nc, hidden), F32)
        for j in range(CMP_STRIDE):
            xj = x_ref[slab, pl.ds(j, nc, stride=CMP_STRIDE), :]
            first += jnp.dot((xj + pe_ref[j:j + 1, :]).astype(BF16), w1_ref[j], preferred_element_type=F32)
            second += jnp.dot((xj + pe_ref[CMP_STRIDE + j:CMP_STRIDE + j + 1, :]).astype(BF16),
                              w1_ref[CMP_STRIDE + j], preferred_element_type=F32)
        bm_scr[0:nc, :] = second
        bm_scr[nc:nc + 8, :] = jnp.zeros((8, hidden), F32)
        z = first + bm_scr[1:nc + 1, :] + b1_ref[...]
        hid = jax.nn.gelu(z)
        out = jnp.dot(hid.astype(BF16), w2_ref[...], preferred_element_type=F32)
        if normalize:
            out = _seg_rms(out, seg_ref[...], gain_ref[0:1, :])
        out = jnp.where(lax.broadcasted_iota(jnp.int32, out.shape, 0) < nc - 1, out, 0.0)
        o_ref[:, 4 * LANES * slab:4 * LANES * (slab + 1)] = jnp.dot(
            out.astype(BF16), pslab_ref[...], preferred_element_type=F32).astype(BF16)


def _compress(kcvc, which, pe, w1, b1, w2, gain, *, batch, normalize):
    n = kcvc.shape[1]
    seq = n // batch
    nc = seq // CMP_STRIDE
    hidden = w1.shape[1]
    eye2 = jnp.eye(2, dtype=F32)
    w1j = w1.reshape(CMP_LEN, HEAD_DIM, hidden)
    w1bd = jax.vmap(lambda m: jnp.kron(eye2, m))(w1j).astype(BF16)
    w2bd = jnp.kron(eye2, w2).astype(BF16)
    pe2 = jnp.tile(pe, (1, 2))
    b12 = jnp.tile(b1.reshape(1, hidden), (1, 2))
    gain2 = jnp.zeros((8, LANES), F32).at[0].set(jnp.tile(gain, 2))
    const2 = lambda b: (0, 0)
    return pl.pallas_call(
        functools.partial(_compress_kernel, normalize=normalize),
        out_shape=jax.ShapeDtypeStruct((batch, nc, 8 * LANES), BF16),
        grid=(batch,),
        in_specs=[
            pl.BlockSpec((2, seq, LANES), lambda b: (which, b, 0)),
            pl.BlockSpec((CMP_LEN, LANES), const2),
            pl.BlockSpec((CMP_LEN, LANES, 2 * hidden), lambda b: (0, 0, 0)),
            pl.BlockSpec((1, 2 * hidden), const2),
            pl.BlockSpec((2 * hidden, LANES), const2),
            pl.BlockSpec((LANES, LANES), const2),
            pl.BlockSpec((LANES, 4 * LANES), const2),
            pl.BlockSpec((8, LANES), const2),
        ],
        out_specs=pl.BlockSpec((None, nc, 8 * LANES), lambda b: (b, 0, 0)),
        scratch_shapes=[pltpu.VMEM((nc + 8, 2 * hidden), F32)],
        compiler_params=_cparams(("arbitrary",)),
        name="nsa_compress",
    )(kcvc, pe2, w1bd, b12, w2bd, _segment_matrix(), _placement_matrix(), gain2)


def _nsa_proj_kernel(h_ref, bn_ref, win_ref, seg_ref, qgain_ref, q_ref, g_ref):
    xn = _rms(h_ref[...], bn_ref[...]).astype(BF16)
    proj = jnp.dot(xn, win_ref[...], preferred_element_type=F32)
    nq = q_ref.shape[1]
    seg = seg_ref[...]
    for s in range(nq // LANES):
        slab = proj[:, LANES * s:LANES * (s + 1)]
        q_ref[:, LANES * s:LANES * (s + 1)] = _seg_rms(slab, seg, qgain_ref[...]).astype(BF16)
    g_ref[...] = jax.nn.sigmoid(proj[:, nq:nq + LANES])


def _nsa_proj(h, b_norm, w_in, q_norm, *, t=512):
    n, d = h.shape
    nq = N_HEADS * HEAD_DIM
    win = jnp.zeros((d, nq + LANES), BF16).at[:, :w_in.shape[1]].set(w_in.astype(BF16))
    qgain = jnp.tile(q_norm * (HEAD_DIM ** -0.5 * LOG2E), LANES // HEAD_DIM).reshape(1, LANES)
    const = lambda i: (0, 0)
    return pl.pallas_call(
        _nsa_proj_kernel,
        out_shape=(jax.ShapeDtypeStruct((n, nq), BF16), jax.ShapeDtypeStruct((n, LANES), F32)),
        grid=(n // t,),
        in_specs=[
            pl.BlockSpec((t, d), lambda i: (i, 0)),
            pl.BlockSpec((1, d), const),
            pl.BlockSpec((d, nq + LANES), const),
            pl.BlockSpec((LANES, LANES), const),
            pl.BlockSpec((1, LANES), const),
        ],
        out_specs=(pl.BlockSpec((t, nq), lambda i: (i, 0)), pl.BlockSpec((t, LANES), lambda i: (i, 0))),
        compiler_params=_cparams(("arbitrary",)),
        name="nsa_proj",
    )(h, b_norm.reshape(1, d), win, _segment_matrix(), qgain)


def _nsa_attn_kernel(cmask_ref, q_ref, g_ref, kc_ref, vc_ref, ks_ref, vs_ref, kw_ref, vw_ref, mapm_ref, gsel_ref,
                     o_ref, m_scr, acc_scr, mw_scr, accw_scr, p_scr, alpha_scr, *, tq):
    step = pl.program_id(2)
    q0 = step * tq
    seq = ks_ref.shape[0]
    rows = 2 * tq
    qq = jnp.concatenate([q_ref[:, 0:LANES], q_ref[:, LANES:2 * LANES]], axis=0)
    t = q0 + lax.broadcasted_iota(jnp.int32, (tq, 1), 0)
    t2 = jnp.concatenate([t, t], axis=0)
    lane_t = lax.broadcasted_iota(jnp.int32, (tq, LANES), 1)
    tk = 2 * tq
    lane_k = lax.broadcasted_iota(jnp.int32, (tk, LANES), 1)
    key_row = lax.broadcasted_iota(jnp.int32, (tk, LANES), 0)
    ones_half = (jnp.where(lane_k < HEAD_DIM, 1.0, 0.0).astype(BF16),
                 jnp.where(lane_k >= HEAD_DIM, 1.0, 0.0).astype(BF16))

    chains = [(half, slice(pair * tq, (pair + 1) * tq)) for half in range(2) for pair in range(2)]

    def key_operands(k_blk, k_extra):
        halves = [k_blk[:, :LANES], k_blk[:, LANES:]]
        return [h if k_extra is None else jnp.concatenate([h, k_extra], axis=1) for h in halves]

    def value_operands(v_blk):
        return [jnp.concatenate([v_blk[:, LANES * half:LANES * (half + 1)], ones_half[half]], axis=1)
                for half in range(2)]

    def probs_chain(lhs, rhs, valid, m_ref, half, r):
        s = lax.dot_general(lhs[r], rhs[half], NT_DIMS, preferred_element_type=F32)
        if valid is not None:
            s = jnp.where(valid, s, MASKED)
        m_old = m_ref[half, r, :]
        m_new = jnp.maximum(m_old, jnp.max(s, axis=-1, keepdims=True))
        m_ref[half, r, :] = m_new
        return jnp.exp2(s - m_new[:, 0:1]).astype(BF16), jnp.exp2(m_old - m_new)

    def value_chain(v_aug, acc_ref, half, r, p, alpha):
        acc_ref[half, r, :] = (acc_ref[half, r, :] * jnp.concatenate([alpha, alpha], axis=1)
                               + jnp.dot(p, v_aug[half], preferred_element_type=F32))

    def result(acc_ref):
        acc = acc_ref[0] + acc_ref[1]
        return acc[:, :LANES] / acc[:, LANES:]

    for ref in (m_scr, mw_scr):
        ref[...] = jnp.full(ref.shape, MASKED, F32)
    for ref in (acc_scr, accw_scr):
        ref[...] = jnp.zeros(ref.shape, F32)

    w0 = pl.multiple_of(jnp.clip(q0 - WINDOW, 0, seq - tk), tq)
    dist = t - (w0 + lax.broadcasted_iota(jnp.int32, (1, tk), 1))
    rhs_w = key_operands(kw_ref[pl.ds(w0, tk), :], None)
    v_w = value_operands(vw_ref[pl.ds(w0, tk), :])
    valid_w = (dist >= 0) & (dist < WINDOW)
    for half, r in chains:
        p, alpha = probs_chain(qq, rhs_w, valid_w, mw_scr, half, r)
        value_chain(v_w, accw_scr, half, r, p, alpha)
    o_win = result(accw_scr)

    kc = kc_ref[...]
    vc = vc_ref[...]
    nc = kc.shape[0]
    cmp_end = lax.broadcasted_iota(jnp.int32, (1, nc), 1) * CMP_STRIDE + (CMP_LEN - 1)
    valid_c = cmp_end <= t2
    row_ok = t2 >= CMP_LEN - 1

    def cmp_probs(half):
        s = lax.dot_general(qq, kc[:, LANES * half:LANES * (half + 1)], NT_DIMS, preferred_element_type=F32)
        s = jnp.where(valid_c, s, MASKED)
        e = jnp.exp2(s - jnp.max(s, axis=-1, keepdims=True))
        return e * jnp.where(row_ok, 1.0 / jnp.sum(e, axis=-1, keepdims=True), 0.0)

    p_e = cmp_probs(0)
    p_o = cmp_probs(1)
    o_cmp = (jnp.dot(p_e.astype(BF16), vc[:, :LANES], preferred_element_type=F32)
             + jnp.dot(p_o.astype(BF16), vc[:, LANES:], preferred_element_type=F32))
    p_sum = (p_e[:tq] + p_o[:tq]) + (p_e[tq:] + p_o[tq:])
    hi, lo = _split_bf16(p_sum)
    mapm = mapm_ref[...]
    imp = jnp.dot(hi, mapm, preferred_element_type=F32) + jnp.dot(lo, mapm, preferred_element_type=F32)

    blk = lax.broadcasted_iota(jnp.int32, (LANES, tq), 0)
    blk_f = blk.astype(F32)
    cur = jnp.right_shift(q0 + lax.broadcasted_iota(jnp.int32, (1, tq), 1), SLC_SHIFT)
    forced = (blk == 0) | (blk == cur) | (blk == cur - 1)
    work = jnp.where(forced, jnp.inf, imp.T)
    work = jnp.where(blk > cur, NEG_INF, work)
    sel_t = jnp.zeros((LANES, tq), F32)
    for _ in range(SLC_TOP_N):
        m = jnp.max(work, axis=0, keepdims=True)
        idx = jnp.min(jnp.where(work == m, blk_f, float(LANES)), axis=0, keepdims=True)
        pick = blk_f == idx
        sel_t = jnp.where(pick, 1.0, sel_t)
        work = jnp.where(pick, NEG_INF, work)
    sel = sel_t.T

    unsel = jnp.where(sel > 0.5, 0.0, cmask_ref[0]).astype(BF16)
    lhs = jnp.concatenate([qq, jnp.concatenate([unsel, unsel], axis=0)], axis=1)

    def sel_keys(j):
        k0 = pl.multiple_of(j * tk, tk)
        key_block = jnp.where(jnp.right_shift(k0 + key_row, SLC_SHIFT) == lane_k, -1.0, 0.0).astype(BF16)
        return k0, key_operands(ks_ref[pl.ds(k0, tk), :], key_block)

    last = (step + 2) // 2 - 1
    k0, rhs = sel_keys(last)
    valid = (k0 + lax.broadcasted_iota(jnp.int32, (1, tk), 1)) <= t
    for half, r in chains:
        p_scr[0, half, r, :], alpha_scr[0, half, r, :] = probs_chain(lhs, rhs, valid, m_scr, half, r)

    def weigh(prev, slot):
        v_prev = value_operands(vs_ref[pl.ds(pl.multiple_of(prev * tk, tk), tk), :])
        return lambda half, r: value_chain(v_prev, acc_scr, half, r, p_scr[slot, half, r, :],
                                           alpha_scr[slot, half, r, :])

    def full_tile(j, prev):
        _, rhs = sel_keys(j)
        weigh_chain = weigh(prev, j & 1)
        for half, r in chains:
            p_scr[(j + 1) & 1, half, r, :], alpha_scr[(j + 1) & 1, half, r, :] = probs_chain(
                lhs, rhs, None, m_scr, half, r)
        for half, r in chains:
            weigh_chain(half, r)
        return j

    prev = lax.fori_loop(0, last, full_tile, last)
    weigh_chain = weigh(prev, last & 1)
    for half, r in chains:
        weigh_chain(half, r)
    o_sel = result(acc_scr)

    gh, gl = _split_bf16(g_ref[...])
    gsel = gsel_ref[...]
    gx = jnp.dot(gh, gsel, preferred_element_type=F32) + jnp.dot(gl, gsel, preferred_element_type=F32)
    out = jnp.zeros((rows, LANES), F32)
    for br, o_br in enumerate((o_cmp, o_sel, o_win)):
        gate = jnp.concatenate([gx[:, LANES * br:LANES * (br + 1)],
                                gx[:, LANES * (N_BRANCH + br):LANES * (N_BRANCH + br + 1)]], axis=0)
        out = out + gate * o_br
    o_ref[:, 0:LANES] = out[:tq].astype(o_ref.dtype)
    o_ref[:, LANES:2 * LANES] = out[tq:].astype(o_ref.dtype)


def _importance_map(nc, n_cmp):
    ratio, span = SLC_LEN // CMP_STRIDE, CMP_LEN // CMP_STRIDE
    w = np.convolve(np.ones(ratio), np.ones(span))
    m = np.zeros((nc, LANES), np.float32)
    for j in range(LANES):
        for o, wt in enumerate(w):
            c = ratio * j + o - (span - 1)
            if 0 <= c < n_cmp:
                m[c, j] = wt
    return jnp.asarray(m, BF16)


def _gate_select():
    m = np.zeros((N_KV_GROUPS, LANES, 2 * N_BRANCH * LANES), np.float32)
    for g in range(N_KV_GROUPS):
        for pair in range(2):
            for br in range(N_BRANCH):
                for lane in range(LANES):
                    r = 2 * pair + lane // HEAD_DIM
                    m[g, (g * HEADS_PER_GROUP + r) * N_BRANCH + br, (pair * N_BRANCH + br) * LANES + lane] = 1.0
    return jnp.asarray(m, BF16)


def _nsa_attention(q, gates, shared, q_norm, k_norm, *, batch, tq=512):
    kcmp, vcmp, ks, vs, kw, vw = shared
    n = q.shape[0]
    seq = n // batch
    assert WINDOW == tq and seq % (2 * tq) == 0
    nq = seq // tq
    nc = kcmp.shape[1]
    width = 2 * LANES
    ks, vs, kw, vw = (a.reshape(batch, seq, N_KV_GROUPS * width) for a in (ks, vs, kw, vw))
    bound = 1.02 * LOG2E * HEAD_DIM ** 0.5 * jnp.max(jnp.abs(q_norm)) * jnp.max(jnp.abs(k_norm[1]))
    cmask = jnp.exp2(jnp.ceil(jnp.log2(2.0 * bound + 1000.0))).reshape(1).astype(F32)
    qrow = lambda b, g, i, *_: (b * nq + i, g)
    per_bg = lambda b, g, i, *_: (b, 0, g)
    resident = lambda: pl.BlockSpec((None, seq, width), per_bg, pipeline_mode=pl.Buffered(1))
    stat = pltpu.VMEM((2, 2 * tq, LANES), F32)
    accum = pltpu.VMEM((2, 2 * tq, 2 * LANES), F32)
    return pl.pallas_call(
        functools.partial(_nsa_attn_kernel, tq=tq),
        out_shape=jax.ShapeDtypeStruct((n, N_HEADS * HEAD_DIM), BF16),
        grid_spec=pltpu.PrefetchScalarGridSpec(
            num_scalar_prefetch=1,
            grid=(batch, N_KV_GROUPS, nq),
            in_specs=[
                pl.BlockSpec((tq, width), qrow),
                pl.BlockSpec((tq, LANES), lambda b, g, i, *_: (b * nq + i, 0)),
                pl.BlockSpec((None, nc, width), per_bg),
                pl.BlockSpec((None, nc, width), per_bg),
                resident(), resident(), resident(), resident(),
                pl.BlockSpec((nc, LANES), lambda b, g, i, *_: (0, 0)),
                pl.BlockSpec((None, LANES, 2 * N_BRANCH * LANES), lambda b, g, i, *_: (g, 0, 0)),
            ],
            out_specs=pl.BlockSpec((tq, width), qrow),
            scratch_shapes=[stat, accum, stat, accum, pltpu.VMEM((2, 2, 2 * tq, 2 * tq), BF16),
                            pltpu.VMEM((2, 2, 2 * tq, LANES), F32)],
        ),
        compiler_params=_cparams(("arbitrary", "arbitrary", "arbitrary")),
        name="nsa_attention",
    )(cmask, q, gates, kcmp, vcmp, ks, vs, kw, vw,
      _importance_map(nc, (seq - CMP_LEN) // CMP_STRIDE + 1), _gate_select())


def _out_proj_kernel(h_ref, o_ref, w_ref, out_ref):
    out_ref[...] = h_ref[...] + jnp.dot(o_ref[...], w_ref[...], preferred_element_type=F32)


def _out_proj(h, o, w_out, *, t=512):
    n, d = h.shape
    k = o.shape[1]
    return pl.pallas_call(
        _out_proj_kernel,
        out_shape=jax.ShapeDtypeStruct((n, d), F32),
        grid=(n // t,),
        in_specs=[pl.BlockSpec((t, d), lambda i: (i, 0)), pl.BlockSpec((t, k), lambda i: (i, 0)),
                  pl.BlockSpec((k, d), lambda i: (0, 0))],
        out_specs=pl.BlockSpec((t, d), lambda i: (i, 0)),
        compiler_params=_cparams(("arbitrary",)),
        name="nsa_out_proj",
    )(h, o, w_out.astype(BF16))


def _nsa_shared_kv(x, kv_norm, w_kv, cmp_pe, phi_w1, phi_b1, phi_w2, k_norm, *, batch):
    kcvc, ks, vs, kw, vw = _kv_proj(x, kv_norm, w_kv, k_norm)
    kcmp = _compress(kcvc, 0, cmp_pe[0], phi_w1[0], phi_b1[0], phi_w2[0], k_norm[0], batch=batch, normalize=True)
    vcmp = _compress(kcvc, 1, cmp_pe[1], phi_w1[1], phi_b1[1], phi_w2[1], k_norm[0], batch=batch, normalize=False)
    return kcmp, vcmp, ks, vs, kw, vw


def _nsa_layer(h, b_norm, w_in, q_norm, w_out, k_norm, shared, *, batch):
    q, gates = _nsa_proj(h, b_norm, w_in, q_norm)
    o = _nsa_attention(q, gates, shared, q_norm, k_norm, batch=batch)
    return _out_proj(h, o, w_out)


def kernel(x, a_norm, a_w_in, a_dw, a_dw_bias, a_conv_norm, a_w_out, kv_norm, w_kv, cmp_pe, phi_w1, phi_b1, phi_w2, k_norm, b_norm, b_w_in, b_q_norm, b_w_out, ffn_norm, router_w, router_bias, moe_w1, moe_w3, moe_w2):
    bsz, s, d = x.shape
    h = x.reshape(bsz * s, d)
    wr = _router_weights(router_w)
    n_a = a_norm.shape[0]
    depth = ffn_norm.shape[0]
    shared = None
    for layer in range(depth):
        if layer < n_a:
            i = layer
            h = _conv_layer(h, a_norm[i], a_w_in[i], a_dw[i], a_dw_bias[i], a_conv_norm[i], a_w_out[i], batch=bsz)
        else:
            i = layer - n_a
            h = _nsa_layer(h, b_norm[i], b_w_in[i], b_q_norm[i], b_w_out[i], k_norm, shared, batch=bsz)
        h = _moe_layer(h, ffn_norm[layer], wr, router_bias, moe_w1[layer].astype(BF16),
                       moe_w3[layer].astype(BF16), moe_w2[layer].astype(BF16))
        if layer == n_a - 1:
            shared = _nsa_shared_kv(h, kv_norm, w_kv, cmp_pe, phi_w1, phi_b1, phi_w2, k_norm, batch=bsz)
    return h.reshape(bsz, s, d)
```

```python
import functools

import jax
import jax.numpy as jnp
import numpy as np
from jax import lax
from jax.experimental import pallas as pl
from jax.experimental.pallas import tpu as pltpu

F32 = jnp.float32
BF16 = jnp.bfloat16

EPS = 1e-6
CONV_WIDTH = 31
CONV_PAD = 32
VMEM_LIMIT = 56 * 1024 * 1024


def _rms(x, g):
    ms = jnp.mean(x * x, axis=-1, keepdims=True)
    return x * lax.rsqrt(ms + EPS) * g


def _cparams(sem):
    return pltpu.CompilerParams(dimension_semantics=sem, vmem_limit_bytes=VMEM_LIMIT)


CONV_ROWS = 64
CONV_LANES = 256


def _conv_layer_kernel(h_ref, an_ref, win_ref, dw_ref, dwb_ref, cn_ref, wout_ref, o_ref, ubuf, ybuf):
    ts, c = ybuf.shape
    t = pl.program_id(1)

    @pl.when(t == 0)
    def _():
        ubuf[:, 0:CONV_PAD, :] = jnp.zeros((8, CONV_PAD, c), F32)

    @pl.when(t != 0)
    def _():
        ubuf[:, 0:CONV_PAD, :] = ubuf[:, ts:ts + CONV_PAD, :]

    x = h_ref[...]
    xn = _rms(x, an_ref[...]).astype(BF16)
    ab = jnp.dot(xn, win_ref[...], preferred_element_type=F32)
    u = ab[:, :c] * jax.nn.sigmoid(ab[:, c:])
    for rho in range(8):
        ubuf[rho, CONV_PAD - rho:CONV_PAD - rho + ts, :] = u

    base_off = CONV_PAD - (CONV_WIDTH - 1)

    def chunk(rc, carry):
        r0 = pl.multiple_of(rc * CONV_ROWS, CONV_ROWS)
        for lc in range(c // CONV_LANES):
            lanes = slice(lc * CONV_LANES, (lc + 1) * CONV_LANES)
            acc = jnp.broadcast_to(dwb_ref[:, lanes], (CONV_ROWS, CONV_LANES))
            for rho in range(8):
                taps = [k for k in range(CONV_WIDTH) if (base_off + k) % 8 == rho]
                first = base_off + taps[0] - rho
                nrows = CONV_ROWS + (taps[-1] - taps[0])
                blk = ubuf[rho, pl.ds(r0 + first, nrows), lanes]
                for k in taps:
                    a = k - taps[0]
                    acc = acc + dw_ref[k:k + 1, lanes] * blk[a:a + CONV_ROWS, :]
            ybuf[pl.ds(r0, CONV_ROWS), lanes] = acc
        return carry

    lax.fori_loop(0, ts // CONV_ROWS, chunk, 0)

    y = ybuf[...]
    yn = _rms(y, cn_ref[...])
    act = (yn * jax.nn.sigmoid(yn)).astype(BF16)
    o_ref[...] = x + jnp.dot(act, wout_ref[...], preferred_element_type=F32)


def _conv_layer(h, a_norm, w_in, dw, dw_bias, conv_norm, w_out, *, batch, ts=256):
    n, d = h.shape
    c = w_out.shape[0]
    seq = n // batch
    nts = seq // ts
    dwp = jnp.zeros((CONV_PAD, c), F32).at[:CONV_WIDTH].set(dw)
    row = lambda b, t: (b * nts + t, 0)
    const = lambda b, t: (0, 0)
    return pl.pallas_call(
        _conv_layer_kernel,
        out_shape=jax.ShapeDtypeStruct((n, d), F32),
        grid=(batch, nts),
        in_specs=[
            pl.BlockSpec((ts, d), row),
            pl.BlockSpec((1, d), const),
            pl.BlockSpec((d, 2 * c), const),
            pl.BlockSpec((CONV_PAD, c), const),
            pl.BlockSpec((1, c), const),
            pl.BlockSpec((1, c), const),
            pl.BlockSpec((c, d), const),
        ],
        out_specs=pl.BlockSpec((ts, d), row),
        scratch_shapes=[pltpu.VMEM((8, ts + CONV_PAD, c), F32), pltpu.VMEM((ts, c), F32)],
        compiler_params=_cparams(("arbitrary", "arbitrary")),
        name="conv_layer",
    )(h, a_norm.reshape(1, d), w_in.astype(BF16), dwp, dw_bias.reshape(1, c),
      conv_norm.reshape(1, c), w_out.astype(BF16))


N_EXPERTS = 16
N_GROUPS = 4
GROUP_SIZE = N_EXPERTS // N_GROUPS
PAIRS = ((0, 1), (0, 2), (0, 3), (1, 2), (1, 3), (2, 3))
N_BUCKETS = N_GROUPS * len(PAIRS)
BUCKET_ROWS = 32
LANES = 128
NEG_INF = float("-inf")


def _split_bf16(x):
    hi = x.astype(BF16)
    lo = (x - hi.astype(F32)).astype(BF16)
    return hi, lo


def _router_logits(xn, wr_ref):
    hi, lo = _split_bf16(xn)
    w = wr_ref[...]
    both = jnp.dot(hi, w, preferred_element_type=F32)
    low = jnp.dot(lo, w[:, :LANES], preferred_element_type=F32)
    return both[:, :LANES] + both[:, LANES:] + low, hi


def _route_kernel(bias_ref, h_ref, fn_ref, wr_ref, tri_ref, bucket_ref, rank_ref, counts_ref, carry):
    tr = h_ref.shape[0]

    @pl.when(pl.program_id(0) == 0)
    def _():
        carry[...] = jnp.zeros_like(carry)

    xn = _rms(h_ref[...], fn_ref[...])
    logits, _ = _router_logits(xn, wr_ref)
    lt = logits.T
    score = jax.nn.sigmoid(lt[0:N_EXPERTS, :])
    biased = [score[e:e + 1, :] + bias_ref[e] for e in range(N_EXPERTS)]

    best = None
    for g in range(N_GROUPS):
        v = biased[GROUP_SIZE * g:GROUP_SIZE * (g + 1)]
        gs = None
        for a, b in PAIRS:
            gs = v[a] + v[b] if gs is None else jnp.maximum(gs, v[a] + v[b])
        if best is None:
            best, gsel = gs, jnp.zeros_like(gs, dtype=jnp.int32)
        else:
            better = gs > best
            gsel = jnp.where(better, g, gsel)
            best = jnp.where(better, gs, best)

    vb = []
    for i in range(GROUP_SIZE):
        val = biased[i]
        for g in range(1, N_GROUPS):
            val = jnp.where(gsel == g, biased[GROUP_SIZE * g + i], val)
        vb.append(val)

    def first_argmax(vals):
        m, idx = vals[0], jnp.zeros_like(gsel)
        for i in range(1, GROUP_SIZE):
            better = vals[i] > m
            idx = jnp.where(better, i, idx)
            m = jnp.where(better, vals[i], m)
        return idx

    i1 = first_argmax(vb)
    i2 = first_argmax([jnp.where(i1 == i, NEG_INF, vb[i]) for i in range(GROUP_SIZE)])
    lo = jnp.minimum(i1, i2)
    hi = jnp.maximum(i1, i2)
    pair = jnp.where(lo == 0, 0, jnp.where(lo == 1, 3, 5)) + hi - lo - 1
    bucket = gsel * len(PAIRS) + pair

    onehot = lax.broadcasted_iota(jnp.int32, (BUCKET_ROWS, tr), 0) == bucket
    before = jnp.dot(onehot.astype(BF16), tri_ref[...], preferred_element_type=F32)
    onef = onehot.astype(F32)
    rank = jnp.sum(onef * (before + carry[:, 0:1]), axis=0, keepdims=True)
    carry[...] = carry[...] + jnp.sum(onef, axis=1, keepdims=True)
    bucket_ref[0] = bucket
    rank_ref[0] = rank.astype(jnp.int32)
    counts_ref[...] = carry[...]


def _moe_route(h, fn, wr, bias, *, tr=512):
    n, d = h.shape
    nt = n // tr
    tri = jnp.triu(jnp.ones((tr, tr), F32), 1).astype(BF16)
    const = lambda i, *_: (0, 0)
    bucket, rank, counts = pl.pallas_call(
        _route_kernel,
        out_shape=(jax.ShapeDtypeStruct((nt, 1, tr), jnp.int32),
                   jax.ShapeDtypeStruct((nt, 1, tr), jnp.int32),
                   jax.ShapeDtypeStruct((BUCKET_ROWS, LANES), F32)),
        grid_spec=pltpu.PrefetchScalarGridSpec(
            num_scalar_prefetch=1,
            grid=(nt,),
            in_specs=[
                pl.BlockSpec((tr, d), lambda i, *_: (i, 0)),
                pl.BlockSpec((1, d), const),
                pl.BlockSpec((d, 2 * LANES), const),
                pl.BlockSpec((tr, tr), const),
            ],
            out_specs=[
                pl.BlockSpec((1, 1, tr), lambda i, *_: (i, 0, 0)),
                pl.BlockSpec((1, 1, tr), lambda i, *_: (i, 0, 0)),
                pl.BlockSpec((BUCKET_ROWS, LANES), const),
            ],
            scratch_shapes=[pltpu.VMEM((BUCKET_ROWS, LANES), F32)],
        ),
        compiler_params=_cparams(("arbitrary",)),
        name="moe_route",
    )(bias, h, fn, wr, tri)
    return bucket, rank, counts


def _scatter_kernel(offs_ref, bucket_ref, rank_ref, src_ref, init_ref, dst_ref, sem, *, ch):
    del init_ref
    for j in range(ch):
        row = offs_ref[bucket_ref[0, 0, j]] + rank_ref[0, 0, j]
        pltpu.make_async_copy(src_ref.at[pl.ds(j, 1)], dst_ref.at[pl.ds(row, 1)], sem).start(priority=j % 2)
    pltpu.make_async_copy(src_ref, dst_ref.at[pl.ds(0, ch)], sem).wait()


def _gather_kernel(offs_ref, bucket_ref, rank_ref, src_ref, dst_ref, sem, *, ch):
    for j in range(ch):
        row = offs_ref[bucket_ref[0, 0, j]] + rank_ref[0, 0, j]
        pltpu.make_async_copy(src_ref.at[pl.ds(row, 1)], dst_ref.at[pl.ds(j, 1)], sem).start(priority=j % 2)
    pltpu.make_async_copy(src_ref.at[pl.ds(0, ch)], dst_ref, sem).wait()


def _permute_rows(src, offs, bucket, rank, *, n_out, scatter, ch=256):
    n = bucket.size
    d = src.shape[1]
    bucket = bucket.reshape(n // ch, 1, ch)
    rank = rank.reshape(n // ch, 1, ch)
    smem_blk = pl.BlockSpec((1, 1, ch), lambda i, *_: (i, 0, 0), memory_space=pltpu.SMEM)
    any_spec = pl.BlockSpec(memory_space=pl.ANY)
    tile_spec = pl.BlockSpec((ch, d), lambda i, *_: (i, 0))
    if scatter:
        body, args, in_specs, out_spec, aliases = (
            _scatter_kernel, [src, jnp.zeros((n_out, d), src.dtype)], [tile_spec, any_spec], any_spec, {4: 0})
    else:
        body, args, in_specs, out_spec, aliases = _gather_kernel, [src], [any_spec], tile_spec, {}
    return pl.pallas_call(
        functools.partial(body, ch=ch),
        out_shape=jax.ShapeDtypeStruct((n_out, d), src.dtype),
        grid_spec=pltpu.PrefetchScalarGridSpec(
            num_scalar_prefetch=1,
            grid=(n // ch,),
            in_specs=[smem_blk, smem_blk] + in_specs,
            out_specs=out_spec,
            scratch_shapes=[pltpu.SemaphoreType.DMA(())],
        ),
        input_output_aliases=aliases,
        compiler_params=_cparams(("arbitrary",)),
        name="moe_scatter" if scatter else "moe_gather",
    )(offs, bucket, rank, *args)


def _ffn_kernel(ea_ref, eb_ref, nv_ref, hs_ref, fn_ref, wr_ref, w1a, w3a, w2a, w1b, w3b, w2b, o_ref):
    i = pl.program_id(0)

    @pl.when(nv_ref[i] == 0)
    def _():
        o_ref[...] = hs_ref[...]

    @pl.when(nv_ref[i] > 0)
    def _():
        x = hs_ref[...]
        xn = _rms(x, fn_ref[...])
        logits, xb = _router_logits(xn, wr_ref)
        score = jax.nn.sigmoid(logits)
        lane = lax.broadcasted_iota(jnp.int32, score.shape, 1)
        sa = jnp.sum(jnp.where(lane == ea_ref[i], score, 0.0), axis=-1, keepdims=True)
        sb = jnp.sum(jnp.where(lane == eb_ref[i], score, 0.0), axis=-1, keepdims=True)
        tot = sa + sb

        def expert(w1, w3, w2, gate):
            h1 = jnp.dot(xb, w1[...], preferred_element_type=F32)
            h3 = jnp.dot(xb, w3[...], preferred_element_type=F32)
            act = (h1 * jax.nn.sigmoid(h1)) * h3 * gate
            return jnp.dot(act.astype(BF16), w2[...], preferred_element_type=F32)

        o_ref[...] = x + expert(w1a, w3a, w2a, sa / tot) + expert(w1b, w3b, w2b, sb / tot)


def _moe_ffn(hs, fn, wr, w1, w3, w2, tile_ea, tile_eb, tile_nv, *, tm):
    ns, d = hs.shape
    de = w1.shape[2]
    ntiles = ns // tm
    const = lambda i, *_: (0, 0)
    wa = lambda i, ea, eb, nv: (ea[i], 0, 0)
    wb = lambda i, ea, eb, nv: (eb[i], 0, 0)
    return pl.pallas_call(
        _ffn_kernel,
        out_shape=jax.ShapeDtypeStruct((ns, d), F32),
        grid_spec=pltpu.PrefetchScalarGridSpec(
            num_scalar_prefetch=3,
            grid=(ntiles,),
            in_specs=[
                pl.BlockSpec((tm, d), lambda i, *_: (i, 0)),
                pl.BlockSpec((1, d), const),
                pl.BlockSpec((d, 2 * LANES), const),
                pl.BlockSpec((None, d, de), wa),
                pl.BlockSpec((None, d, de), wa),
                pl.BlockSpec((None, de, d), wa),
                pl.BlockSpec((None, d, de), wb),
                pl.BlockSpec((None, d, de), wb),
                pl.BlockSpec((None, de, d), wb),
            ],
            out_specs=pl.BlockSpec((tm, d), lambda i, *_: (i, 0)),
        ),
        compiler_params=_cparams(("arbitrary",)),
        name="moe_ffn",
    )(tile_ea, tile_eb, tile_nv, hs, fn, wr, w1, w3, w2, w1, w3, w2)


def _router_weights(router_w):
    d = router_w.shape[0]
    hi, lo = _split_bf16(router_w)
    wr = jnp.zeros((d, 2 * LANES), BF16)
    return wr.at[:, :N_EXPERTS].set(hi).at[:, LANES:LANES + N_EXPERTS].set(lo)


def _moe_layer(h, fn, wr, bias, w1, w3, w2, *, tm=512):
    n, d = h.shape
    fn = fn.reshape(1, d)
    bucket, rank, counts = _moe_route(h, fn, wr, bias)
    counts = counts[:N_BUCKETS, 0].astype(jnp.int32)
    ntile_b = (counts + tm - 1) // tm
    tile_end = jnp.cumsum(ntile_b)
    tile_start = tile_end - ntile_b
    offs = jnp.zeros((BUCKET_ROWS,), jnp.int32).at[:N_BUCKETS].set(tile_start * tm)
    ntiles = n // tm + N_BUCKETS
    j = jnp.arange(ntiles, dtype=jnp.int32)
    used = j < tile_end[-1]
    tb = jnp.minimum(jnp.searchsorted(tile_end, j, side="right").astype(jnp.int32), N_BUCKETS - 1)
    last_b = jnp.max(jnp.where(ntile_b > 0, jnp.arange(N_BUCKETS, dtype=jnp.int32), 0))
    tb = jnp.where(used, tb, last_b)
    nv = jnp.where(used, jnp.clip(counts[tb] - (j - tile_start[tb]) * tm, 0, tm), 0).astype(jnp.int32)
    pair_lo = jnp.asarray([p[0] for p in PAIRS], jnp.int32)
    pair_hi = jnp.asarray([p[1] for p in PAIRS], jnp.int32)
    ea = GROUP_SIZE * (tb // len(PAIRS)) + pair_lo[tb % len(PAIRS)]
    eb = GROUP_SIZE * (tb // len(PAIRS)) + pair_hi[tb % len(PAIRS)]
    hs = _permute_rows(h, offs, bucket, rank, n_out=ntiles * tm, scatter=True)
    ys = _moe_ffn(hs, fn, wr, w1, w3, w2, ea, eb, nv, tm=tm)
    return _permute_rows(ys, offs, bucket, rank, n_out=n, scatter=False)


N_HEADS = 16
N_KV_GROUPS = 4
HEADS_PER_GROUP = N_HEADS // N_KV_GROUPS
HEAD_DIM = 64
CMP_LEN = 32
CMP_STRIDE = 16
SLC_LEN = 64
SLC_SHIFT = 6
SLC_TOP_N = 16
WINDOW = 512
N_BRANCH = 3
MASKED = -1e30
LOG2E = 1.4426950408889634
NT_DIMS = (((1,), (1,)), ((), ()))


def _segment_matrix():
    seg = np.kron(np.eye(LANES // HEAD_DIM), np.full((HEAD_DIM, HEAD_DIM), 1.0 / HEAD_DIM))
    return jnp.asarray(seg, BF16)


def _placement_matrix():
    p = np.zeros((LANES, 4 * LANES), np.float32)
    for half in range(2):
        for l in range(HEAD_DIM):
            p[half * HEAD_DIM + l, half * 2 * LANES + l] = 1.0
            p[half * HEAD_DIM + l, half * 2 * LANES + 3 * HEAD_DIM + l] = 1.0
    return jnp.asarray(p, BF16)


def _seg_rms(x, seg, gain):
    hi, lo = _split_bf16(x * x)
    ms = jnp.dot(hi, seg, preferred_element_type=F32) + jnp.dot(lo, seg, preferred_element_type=F32)
    return x * lax.rsqrt(ms + EPS) * gain


def _kv_proj_kernel(x_ref, kvn_ref, wkv_ref, seg_ref, pslab_ref, kgain_ref,
                    kcvc_ref, ks_ref, vs_ref, kw_ref, vw_ref):
    xn = _rms(x_ref[...], kvn_ref[...]).astype(BF16)
    kv = jnp.dot(xn, wkv_ref[...], preferred_element_type=F32)
    width = N_KV_GROUPS * HEAD_DIM
    for s in range(4):
        kcvc_ref[s] = kv[:, LANES * s:LANES * (s + 1)]
    seg = seg_ref[...]
    pslab = pslab_ref[...]

    def place(i, out_ref, gain_row):
        for s in range(width // LANES):
            slab = kv[:, width * i + LANES * s:width * i + LANES * (s + 1)]
            if gain_row is not None:
                slab = _seg_rms(slab, seg, kgain_ref[gain_row:gain_row + 1, :])
            out_ref[:, 4 * LANES * s:4 * LANES * (s + 1)] = jnp.dot(
                slab.astype(BF16), pslab, preferred_element_type=F32).astype(BF16)

    place(2, ks_ref, 1)
    place(3, vs_ref, None)
    place(4, kw_ref, 2)
    place(5, vw_ref, None)


def _kv_proj(x, kv_norm, w_kv, k_norm, *, t=512):
    n, d = x.shape
    width = N_KV_GROUPS * HEAD_DIM
    kgain = jnp.zeros((8, LANES), F32).at[:N_BRANCH].set(jnp.tile(k_norm, (1, LANES // HEAD_DIM)))
    const = lambda i: (0, 0)
    wide = jax.ShapeDtypeStruct((n, 4 * width), BF16)
    wide_spec = pl.BlockSpec((t, 4 * width), lambda i: (i, 0))
    return pl.pallas_call(
        _kv_proj_kernel,
        out_shape=(jax.ShapeDtypeStruct((4, n, LANES), F32), wide, wide, wide, wide),
        grid=(n // t,),
        in_specs=[
            pl.BlockSpec((t, d), lambda i: (i, 0)),
            pl.BlockSpec((1, d), const),
            pl.BlockSpec((d, 6 * width), const),
            pl.BlockSpec((LANES, LANES), const),
            pl.BlockSpec((LANES, 4 * LANES), const),
            pl.BlockSpec((8, LANES), const),
        ],
        out_specs=(pl.BlockSpec((4, t, LANES), lambda i: (0, i, 0)), wide_spec, wide_spec, wide_spec, wide_spec),
        compiler_params=_cparams(("arbitrary",)),
        name="nsa_kv_proj",
    )(x, kv_norm.reshape(1, d), w_kv.astype(BF16), _segment_matrix(), _placement_matrix(), kgain)


def _compress_kernel(x_ref, pe_ref, w1_ref, b1_ref, w2_ref, seg_ref, pslab_ref, gain_ref, o_ref, bm_scr,
                     *, normalize):
    seq = x_ref.shape[1]
    nc = seq // CMP_STRIDE
    halves = CMP_LEN // CMP_STRIDE
    assert halves == 2
    hidden = w1_ref.shape[2]
    for slab in range(2):
        first = jnp.zeros((nc, hidden), F32)
        second = jnp.zeros((nc, hidden), F32)
        for j in range(CMP_STRIDE):
            xj = x_ref[slab, pl.ds(j, nc, stride=CMP_STRIDE), :]
            first += jnp.dot((xj + pe_ref[j:j + 1, :]).astype(BF16), w1_ref[j], preferred_element_type=F32)
            second += jnp.dot((xj + pe_ref[CMP_STRIDE + j:CMP_STRIDE + j + 1, :]).astype(BF16),
                              w1_ref[CMP_STRIDE + j], preferred_element_type=F32)
        bm_scr[0:nc, :] = second
        bm_scr[nc:nc + 8, :] = jnp.zeros((8, hidden), F32)
        z = first + bm_scr[1:nc + 1, :] + b1_ref[...]
        hid = jax.nn.gelu(z)
        out = jnp.dot(hid.astype(BF16), w2_ref[...], preferred_element_type=F32)
        if normalize:
            out = _seg_rms(out, seg_ref[...], gain_ref[0:1, :])
        out = jnp.where(lax.broadcasted_iota(jnp.int32, out.shape, 0) < nc - 1, out, 0.0)
        o_ref[:, 4 * LANES * slab:4 * LANES * (slab + 1)] = jnp.dot(
            out.astype(BF16), pslab_ref[...], preferred_element_type=F32).astype(BF16)


def _compress(kcvc, which, pe, w1, b1, w2, gain, *, batch, normalize):
    n = kcvc.shape[1]
    seq = n // batch
    nc = seq // CMP_STRIDE
    hidden = w1.shape[1]
    eye2 = jnp.eye(2, dtype=F32)
    w1j = w1.reshape(CMP_LEN, HEAD_DIM, hidden)
    w1bd = jax.vmap(lambda m: jnp.kron(eye2, m))(w1j).astype(BF16)
    w2bd = jnp.kron(eye2, w2).astype(BF16)
    pe2 = jnp.tile(pe, (1, 2))
    b12 = jnp.tile(b1.reshape(1, hidden), (1, 2))
    gain2 = jnp.zeros((8, LANES), F32).at[0].set(jnp.tile(gain, 2))
    const2 = lambda b: (0, 0)
    return pl.pallas_call(
        functools.partial(_compress_kernel, normalize=normalize),
        out_shape=jax.ShapeDtypeStruct((batch, nc, 8 * LANES), BF16),
        grid=(batch,),
        in_specs=[
            pl.BlockSpec((2, seq, LANES), lambda b: (which, b, 0)),
            pl.BlockSpec((CMP_LEN, LANES), const2),
            pl.BlockSpec((CMP_LEN, LANES, 2 * hidden), lambda b: (0, 0, 0)),
            pl.BlockSpec((1, 2 * hidden), const2),
            pl.BlockSpec((2 * hidden, LANES), const2),
            pl.BlockSpec((LANES, LANES), const2),
            pl.BlockSpec((LANES, 4 * LANES), const2),
            pl.BlockSpec((8, LANES), const2),
        ],
        out_specs=pl.BlockSpec((None, nc, 8 * LANES), lambda b: (b, 0, 0)),
        scratch_shapes=[pltpu.VMEM((nc + 8, 2 * hidden), F32)],
        compiler_params=_cparams(("arbitrary",)),
        name="nsa_compress",
    )(kcvc, pe2, w1bd, b12, w2bd, _segment_matrix(), _placement_matrix(), gain2)


def _nsa_proj_kernel(h_ref, bn_ref, win_ref, seg_ref, qgain_ref, q_ref, g_ref):
    xn = _rms(h_ref[...], bn_ref[...]).astype(BF16)
    proj = jnp.dot(xn, win_ref[...], preferred_element_type=F32)
    nq = q_ref.shape[1]
    seg = seg_ref[...]
    for s in range(nq // LANES):
        slab = proj[:, LANES * s:LANES * (s + 1)]
        q_ref[:, LANES * s:LANES * (s + 1)] = _seg_rms(slab, seg, qgain_ref[...]).astype(BF16)
    g_ref[...] = jax.nn.sigmoid(proj[:, nq:nq + LANES])


def _nsa_proj(h, b_norm, w_in, q_norm, *, t=512):
    n, d = h.shape
    nq = N_HEADS * HEAD_DIM
    win = jnp.zeros((d, nq + LANES), BF16).at[:, :w_in.shape[1]].set(w_in.astype(BF16))
    qgain = jnp.tile(q_norm * (HEAD_DIM ** -0.5 * LOG2E), LANES // HEAD_DIM).reshape(1, LANES)
    const = lambda i: (0, 0)
    return pl.pallas_call(
        _nsa_proj_kernel,
        out_shape=(jax.ShapeDtypeStruct((n, nq), BF16), jax.ShapeDtypeStruct((n, LANES), F32)),
        grid=(n // t,),
        in_specs=[
            pl.BlockSpec((t, d), lambda i: (i, 0)),
            pl.BlockSpec((1, d), const),
            pl.BlockSpec((d, nq + LANES), const),
            pl.BlockSpec((LANES, LANES), const),
            pl.BlockSpec((1, LANES), const),
        ],
        out_specs=(pl.BlockSpec((t, nq), lambda i: (i, 0)), pl.BlockSpec((t, LANES), lambda i: (i, 0))),
        compiler_params=_cparams(("arbitrary",)),
        name="nsa_proj",
    )(h, b_norm.reshape(1, d), win, _segment_matrix(), qgain)


def _nsa_attn_kernel(cmask_ref, q_ref, g_ref, kc_ref, vc_ref, ks_ref, vs_ref, kw_ref, vw_ref, mapm_ref, gsel_ref,
                     o_ref, m_scr, acc_scr, *, tq):
    step = pl.program_id(2)
    q0 = step * tq
    seq = ks_ref.shape[0]
    rows = 2 * tq
    qq = jnp.concatenate([q_ref[:, 0:LANES], q_ref[:, LANES:2 * LANES]], axis=0)
    t = q0 + lax.broadcasted_iota(jnp.int32, (tq, 1), 0)
    t2 = jnp.concatenate([t, t], axis=0)
    lane_t = lax.broadcasted_iota(jnp.int32, (tq, LANES), 1)
    tk = 2 * tq
    lane_k = lax.broadcasted_iota(jnp.int32, (tk, LANES), 1)
    key_row = lax.broadcasted_iota(jnp.int32, (tk, LANES), 0)
    ones_half = (jnp.where(lane_k < HEAD_DIM, 1.0, 0.0).astype(BF16),
                 jnp.where(lane_k >= HEAD_DIM, 1.0, 0.0).astype(BF16))

    def reset():
        m_scr[...] = jnp.full(m_scr.shape, MASKED, F32)
        acc_scr[...] = jnp.zeros(acc_scr.shape, F32)

    def attend(lhs, k_blk, v_blk, valid, k_extra=None):
        chains = [(half, slice(pair * tq, (pair + 1) * tq)) for half in range(2) for pair in range(2)]
        probs = {}
        for half, r in chains:
            rhs = k_blk[:, LANES * half:LANES * (half + 1)]
            if k_extra is not None:
                rhs = jnp.concatenate([rhs, k_extra], axis=1)
            s = lax.dot_general(lhs[r], rhs, NT_DIMS, preferred_element_type=F32).astype(BF16)
            if valid is not None:
                s = jnp.where(valid, s, MASKED)
            m_old = m_scr[half, r, :]
            m_new = jnp.maximum(m_old, jnp.max(s, axis=-1, keepdims=True).astype(F32))
            m_scr[half, r, :] = m_new
            probs[half, r.start] = (jnp.exp2(s - m_new[:, 0:1].astype(BF16)), jnp.exp2(m_old - m_new))
        for half, r in chains:
            p, alpha = probs[half, r.start]
            v_aug = jnp.concatenate([v_blk[:, LANES * half:LANES * (half + 1)], ones_half[half]], axis=1)
            acc_scr[half, r, :] = (acc_scr[half, r, :] * jnp.concatenate([alpha, alpha], axis=1)
                                   + jnp.dot(p, v_aug, preferred_element_type=F32))

    def result():
        acc = acc_scr[0] + acc_scr[1]
        return acc[:, :LANES] / acc[:, LANES:]

    kc = kc_ref[...]
    vc = vc_ref[...]
    nc = kc.shape[0]
    cmp_end = lax.broadcasted_iota(jnp.int32, (1, nc), 1) * CMP_STRIDE + (CMP_LEN - 1)
    valid_c = cmp_end <= t2
    row_ok = t2 >= CMP_LEN - 1

    def cmp_probs(half):
        s = lax.dot_general(qq, kc[:, LANES * half:LANES * (half + 1)], NT_DIMS, preferred_element_type=F32)
        s = jnp.where(valid_c, s, MASKED)
        e = jnp.exp2(s - jnp.max(s, axis=-1, keepdims=True))
        return e * jnp.where(row_ok, 1.0 / jnp.sum(e, axis=-1, keepdims=True), 0.0)

    p_e = cmp_probs(0)
    p_o = cmp_probs(1)
    o_cmp = (jnp.dot(p_e.astype(BF16), vc[:, :LANES], preferred_element_type=F32)
             + jnp.dot(p_o.astype(BF16), vc[:, LANES:], preferred_element_type=F32))
    p_sum = (p_e[:tq] + p_o[:tq]) + (p_e[tq:] + p_o[tq:])
    hi, lo = _split_bf16(p_sum)
    mapm = mapm_ref[...]
    imp = jnp.dot(hi, mapm, preferred_element_type=F32) + jnp.dot(lo, mapm, preferred_element_type=F32)

    blk = lax.broadcasted_iota(jnp.int32, (LANES, tq), 0)
    blk_f = blk.astype(F32)
    cur = jnp.right_shift(q0 + lax.broadcasted_iota(jnp.int32, (1, tq), 1), SLC_SHIFT)
    forced = (blk == 0) | (blk == cur) | (blk == cur - 1)
    sel_t = jnp.where(forced, 1.0, 0.0)
    work = jnp.where(forced | (blk > cur), NEG_INF, imp.T)
    for _ in range(SLC_TOP_N - 3):
        m = jnp.max(work, axis=0, keepdims=True)
        idx = jnp.min(jnp.where(work == m, blk_f, float(LANES)), axis=0, keepdims=True)
        pick = blk_f == idx
        sel_t = jnp.where(pick, 1.0, sel_t)
        work = jnp.where(pick, NEG_INF, work)
    sel = sel_t.T

    unsel = jnp.where(sel > 0.5, 0.0, cmask_ref[0]).astype(BF16)
    lhs = jnp.concatenate([qq, jnp.concatenate([unsel, unsel], axis=0)], axis=1)
    reset()

    def key_tile(j, causal):
        k0 = pl.multiple_of(j * tk, tk)
        key_block = jnp.where(jnp.right_shift(k0 + key_row, SLC_SHIFT) == lane_k, -1.0, 0.0).astype(BF16)
        valid = (k0 + lax.broadcasted_iota(jnp.int32, (1, tk), 1)) <= t if causal else None
        attend(lhs, ks_ref[pl.ds(k0, tk), :], vs_ref[pl.ds(k0, tk), :], valid, k_extra=key_block)

    def full_tile(j, carry):
        key_tile(j, False)
        return carry

    last = (step + 2) // 2 - 1
    lax.fori_loop(0, last, full_tile, 0)
    key_tile(last, True)
    o_sel = result()

    reset()
    w0 = pl.multiple_of(jnp.clip(q0 - WINDOW, 0, seq - tk), tq)
    dist = t - (w0 + lax.broadcasted_iota(jnp.int32, (1, tk), 1))
    attend(qq, kw_ref[pl.ds(w0, tk), :], vw_ref[pl.ds(w0, tk), :], (dist >= 0) & (dist < WINDOW))
    o_win = result()

    gh, gl = _split_bf16(g_ref[...])
    gsel = gsel_ref[...]
    gx = jnp.dot(gh, gsel, preferred_element_type=F32) + jnp.dot(gl, gsel, preferred_element_type=F32)
    out = jnp.zeros((rows, LANES), F32)
    for br, o_br in enumerate((o_cmp, o_sel, o_win)):
        gate = jnp.concatenate([gx[:, LANES * br:LANES * (br + 1)],
                                gx[:, LANES * (N_BRANCH + br):LANES * (N_BRANCH + br + 1)]], axis=0)
        out = out + gate * o_br
    o_ref[:, 0:LANES] = out[:tq].astype(o_ref.dtype)
    o_ref[:, LANES:2 * LANES] = out[tq:].astype(o_ref.dtype)


def _importance_map(nc, n_cmp):
    ratio, span = SLC_LEN // CMP_STRIDE, CMP_LEN // CMP_STRIDE
    w = np.convolve(np.ones(ratio), np.ones(span))
    m = np.zeros((nc, LANES), np.float32)
    for j in range(LANES):
        for o, wt in enumerate(w):
            c = ratio * j + o - (span - 1)
            if 0 <= c < n_cmp:
                m[c, j] = wt
    return jnp.asarray(m, BF16)


def _gate_select():
    m = np.zeros((N_KV_GROUPS, LANES, 2 * N_BRANCH * LANES), np.float32)
    for g in range(N_KV_GROUPS):
        for pair in range(2):
            for br in range(N_BRANCH):
                for lane in range(LANES):
                    r = 2 * pair + lane // HEAD_DIM
                    m[g, (g * HEADS_PER_GROUP + r) * N_BRANCH + br, (pair * N_BRANCH + br) * LANES + lane] = 1.0
    return jnp.asarray(m, BF16)


def _nsa_attention(q, gates, shared, q_norm, k_norm, *, batch, tq=512):
    kcmp, vcmp, ks, vs, kw, vw = shared
    n = q.shape[0]
    seq = n // batch
    assert WINDOW == tq and seq % (2 * tq) == 0
    nq = seq // tq
    nc = kcmp.shape[1]
    width = 2 * LANES
    ks, vs, kw, vw = (a.reshape(batch, seq, N_KV_GROUPS * width) for a in (ks, vs, kw, vw))
    bound = 1.02 * LOG2E * HEAD_DIM ** 0.5 * jnp.max(jnp.abs(q_norm)) * jnp.max(jnp.abs(k_norm[1]))
    cmask = jnp.exp2(jnp.ceil(jnp.log2(2.0 * bound + 1000.0))).reshape(1).astype(F32)
    qrow = lambda b, g, i, *_: (b * nq + i, g)
    per_bg = lambda b, g, i, *_: (b, 0, g)
    return pl.pallas_call(
        functools.partial(_nsa_attn_kernel, tq=tq),
        out_shape=jax.ShapeDtypeStruct((n, N_HEADS * HEAD_DIM), BF16),
        grid_spec=pltpu.PrefetchScalarGridSpec(
            num_scalar_prefetch=1,
            grid=(batch, N_KV_GROUPS, nq),
            in_specs=[
                pl.BlockSpec((tq, width), qrow),
                pl.BlockSpec((tq, LANES), lambda b, g, i, *_: (b * nq + i, 0)),
                pl.BlockSpec((None, nc, width), per_bg),
                pl.BlockSpec((None, nc, width), per_bg),
                pl.BlockSpec((None, seq, width), per_bg),
                pl.BlockSpec((None, seq, width), per_bg),
                pl.BlockSpec((None, seq, width), per_bg),
                pl.BlockSpec((None, seq, width), per_bg),
                pl.BlockSpec((nc, LANES), lambda b, g, i, *_: (0, 0)),
                pl.BlockSpec((None, LANES, 2 * N_BRANCH * LANES), lambda b, g, i, *_: (g, 0, 0)),
            ],
            out_specs=pl.BlockSpec((tq, width), qrow),
            scratch_shapes=[pltpu.VMEM((2, 2 * tq, LANES), F32), pltpu.VMEM((2, 2 * tq, 2 * LANES), F32)],
        ),
        compiler_params=_cparams(("arbitrary", "arbitrary", "arbitrary")),
        name="nsa_attention",
    )(cmask, q, gates, kcmp, vcmp, ks, vs, kw, vw,
      _importance_map(nc, (seq - CMP_LEN) // CMP_STRIDE + 1), _gate_select())


def _out_proj_kernel(h_ref, o_ref, w_ref, out_ref):
    out_ref[...] = h_ref[...] + jnp.dot(o_ref[...], w_ref[...], preferred_element_type=F32)


def _out_proj(h, o, w_out, *, t=512):
    n, d = h.shape
    k = o.shape[1]
    return pl.pallas_call(
        _out_proj_kernel,
        out_shape=jax.ShapeDtypeStruct((n, d), F32),
        grid=(n // t,),
        in_specs=[pl.BlockSpec((t, d), lambda i: (i, 0)), pl.BlockSpec((t, k), lambda i: (i, 0)),
                  pl.BlockSpec((k, d), lambda i: (0, 0))],
        out_specs=pl.BlockSpec((t, d), lambda i: (i, 0)),
        compiler_params=_cparams(("arbitrary",)),
        name="nsa_out_proj",
    )(h, o, w_out.astype(BF16))


def _nsa_shared_kv(x, kv_norm, w_kv, cmp_pe, phi_w1, phi_b1, phi_w2, k_norm, *, batch):
    kcvc, ks, vs, kw, vw = _kv_proj(x, kv_norm, w_kv, k_norm)
    kcmp = _compress(kcvc, 0, cmp_pe[0], phi_w1[0], phi_b1[0], phi_w2[0], k_norm[0], batch=batch, normalize=True)
    vcmp = _compress(kcvc, 1, cmp_pe[1], phi_w1[1], phi_b1[1], phi_w2[1], k_norm[0], batch=batch, normalize=False)
    return kcmp, vcmp, ks, vs, kw, vw


def _nsa_layer(h, b_norm, w_in, q_norm, w_out, k_norm, shared, *, batch):
    q, gates = _nsa_proj(h, b_norm, w_in, q_norm)
    o = _nsa_attention(q, gates, shared, q_norm, k_norm, batch=batch)
    return _out_proj(h, o, w_out)


def kernel(x, a_norm, a_w_in, a_dw, a_dw_bias, a_conv_norm, a_w_out, kv_norm, w_kv, cmp_pe, phi_w1, phi_b1, phi_w2, k_norm, b_norm, b_w_in, b_q_norm, b_w_out, ffn_norm, router_w, router_bias, moe_w1, moe_w3, moe_w2):
    bsz, s, d = x.shape
    h = x.reshape(bsz * s, d)
    wr = _router_weights(router_w)
    n_a = a_norm.shape[0]
    depth = ffn_norm.shape[0]
    shared = None
    for layer in range(depth):
        if layer < n_a:
            i = layer
            h = _conv_layer(h, a_norm[i], a_w_in[i], a_dw[i], a_dw_bias[i], a_conv_norm[i], a_w_out[i], batch=bsz)
        else:
            i = layer - n_a
            h = _nsa_layer(h, b_norm[i], b_w_in[i], b_q_norm[i], b_w_out[i], k_norm, shared, batch=bsz)
        h = _moe_layer(h, ffn_norm[layer], wr, router_bias, moe_w1[layer].astype(BF16),
                       moe_w3[layer].astype(BF16), moe_w2[layer].astype(BF16))
        if layer == n_a - 1:
            shared = _nsa_shared_kv(h, kv_norm, w_kv, cmp_pe, phi_w1, phi_b1, phi_w2, k_norm, batch=bsz)
    return h.reshape(bsz, s, d)
```

```python
import functools

import jax
import jax.numpy as jnp
import numpy as np
from jax import lax
from jax.experimental import pallas as pl
from jax.experimental.pallas import tpu as pltpu

F32 = jnp.float32
BF16 = jnp.bfloat16

EPS = 1e-6
CONV_WIDTH = 31
CONV_PAD = 32
VMEM_LIMIT = 56 * 1024 * 1024


def _rms(x, g):
    ms = jnp.mean(x * x, axis=-1, keepdims=True)
    return x * lax.rsqrt(ms + EPS) * g


def _cparams(sem):
    return pltpu.CompilerParams(dimension_semantics=sem, vmem_limit_bytes=VMEM_LIMIT)


CONV_ROWS = 64
CONV_LANES = 256


def _conv_layer_kernel(h_ref, an_ref, win_ref, dw_ref, dwb_ref, cn_ref, wout_ref, o_ref, ubuf, ybuf):
    ts, c = ybuf.shape
    t = pl.program_id(1)

    @pl.when(t == 0)
    def _():
        ubuf[:, 0:CONV_PAD, :] = jnp.zeros((8, CONV_PAD, c), F32)

    @pl.when(t != 0)
    def _():
        ubuf[:, 0:CONV_PAD, :] = ubuf[:, ts:ts + CONV_PAD, :]

    x = h_ref[...]
    xn = _rms(x, an_ref[...]).astype(BF16)
    ab = jnp.dot(xn, win_ref[...], preferred_element_type=F32)
    u = ab[:, :c] * jax.nn.sigmoid(ab[:, c:])
    for rho in range(8):
        ubuf[rho, CONV_PAD - rho:CONV_PAD - rho + ts, :] = u

    base_off = CONV_PAD - (CONV_WIDTH - 1)

    def chunk(rc, carry):
        r0 = pl.multiple_of(rc * CONV_ROWS, CONV_ROWS)
        for lc in range(c // CONV_LANES):
            lanes = slice(lc * CONV_LANES, (lc + 1) * CONV_LANES)
            acc = jnp.broadcast_to(dwb_ref[:, lanes], (CONV_ROWS, CONV_LANES))
            for rho in range(8):
                taps = [k for k in range(CONV_WIDTH) if (base_off + k) % 8 == rho]
                first = base_off + taps[0] - rho
                nrows = CONV_ROWS + (taps[-1] - taps[0])
                blk = ubuf[rho, pl.ds(r0 + first, nrows), lanes]
                for k in taps:
                    a = k - taps[0]
                    acc = acc + dw_ref[k:k + 1, lanes] * blk[a:a + CONV_ROWS, :]
            ybuf[pl.ds(r0, CONV_ROWS), lanes] = acc
        return carry

    lax.fori_loop(0, ts // CONV_ROWS, chunk, 0)

    y = ybuf[...]
    yn = _rms(y, cn_ref[...])
    act = (yn * jax.nn.sigmoid(yn)).astype(BF16)
    o_ref[...] = x + jnp.dot(act, wout_ref[...], preferred_element_type=F32)


def _conv_layer(h, a_norm, w_in, dw, dw_bias, conv_norm, w_out, *, batch, ts=256):
    n, d = h.shape
    c = w_out.shape[0]
    seq = n // batch
    nts = seq // ts
    dwp = jnp.zeros((CONV_PAD, c), F32).at[:CONV_WIDTH].set(dw)
    row = lambda b, t: (b * nts + t, 0)
    const = lambda b, t: (0, 0)
    return pl.pallas_call(
        _conv_layer_kernel,
        out_shape=jax.ShapeDtypeStruct((n, d), F32),
        grid=(batch, nts),
        in_specs=[
            pl.BlockSpec((ts, d), row),
            pl.BlockSpec((1, d), const),
            pl.BlockSpec((d, 2 * c), const),
            pl.BlockSpec((CONV_PAD, c), const),
            pl.BlockSpec((1, c), const),
            pl.BlockSpec((1, c), const),
            pl.BlockSpec((c, d), const),
        ],
        out_specs=pl.BlockSpec((ts, d), row),
        scratch_shapes=[pltpu.VMEM((8, ts + CONV_PAD, c), F32), pltpu.VMEM((ts, c), F32)],
        compiler_params=_cparams(("arbitrary", "arbitrary")),
        name="conv_layer",
    )(h, a_norm.reshape(1, d), w_in.astype(BF16), dwp, dw_bias.reshape(1, c),
      conv_norm.reshape(1, c), w_out.astype(BF16))


N_EXPERTS = 16
N_GROUPS = 4
GROUP_SIZE = N_EXPERTS // N_GROUPS
PAIRS = ((0, 1), (0, 2), (0, 3), (1, 2), (1, 3), (2, 3))
N_BUCKETS = N_GROUPS * len(PAIRS)
BUCKET_ROWS = 32
LANES = 128
NEG_INF = float("-inf")


def _split_bf16(x):
    hi = x.astype(BF16)
    lo = (x - hi.astype(F32)).astype(BF16)
    return hi, lo


def _router_logits(xn, wr_ref):
    hi, lo = _split_bf16(xn)
    w = wr_ref[...]
    both = jnp.dot(hi, w, preferred_element_type=F32)
    low = jnp.dot(lo, w[:, :LANES], preferred_element_type=F32)
    return both[:, :LANES] + both[:, LANES:] + low, hi


def _route_kernel(bias_ref, h_ref, fn_ref, wr_ref, tri_ref, bucket_ref, rank_ref, counts_ref, carry):
    tr = h_ref.shape[0]

    @pl.when(pl.program_id(0) == 0)
    def _():
        carry[...] = jnp.zeros_like(carry)

    xn = _rms(h_ref[...], fn_ref[...])
    logits, _ = _router_logits(xn, wr_ref)
    lt = logits.T
    score = jax.nn.sigmoid(lt[0:N_EXPERTS, :])
    biased = [score[e:e + 1, :] + bias_ref[e] for e in range(N_EXPERTS)]

    best = None
    for g in range(N_GROUPS):
        v = biased[GROUP_SIZE * g:GROUP_SIZE * (g + 1)]
        gs = None
        for a, b in PAIRS:
            gs = v[a] + v[b] if gs is None else jnp.maximum(gs, v[a] + v[b])
        if best is None:
            best, gsel = gs, jnp.zeros_like(gs, dtype=jnp.int32)
        else:
            better = gs > best
            gsel = jnp.where(better, g, gsel)
            best = jnp.where(better, gs, best)

    vb = []
    for i in range(GROUP_SIZE):
        val = biased[i]
        for g in range(1, N_GROUPS):
            val = jnp.where(gsel == g, biased[GROUP_SIZE * g + i], val)
        vb.append(val)

    def first_argmax(vals):
        m, idx = vals[0], jnp.zeros_like(gsel)
        for i in range(1, GROUP_SIZE):
            better = vals[i] > m
            idx = jnp.where(better, i, idx)
            m = jnp.where(better, vals[i], m)
        return idx

    i1 = first_argmax(vb)
    i2 = first_argmax([jnp.where(i1 == i, NEG_INF, vb[i]) for i in range(GROUP_SIZE)])
    lo = jnp.minimum(i1, i2)
    hi = jnp.maximum(i1, i2)
    pair = jnp.where(lo == 0, 0, jnp.where(lo == 1, 3, 5)) + hi - lo - 1
    bucket = gsel * len(PAIRS) + pair

    onehot = lax.broadcasted_iota(jnp.int32, (BUCKET_ROWS, tr), 0) == bucket
    before = jnp.dot(onehot.astype(BF16), tri_ref[...], preferred_element_type=F32)
    onef = onehot.astype(F32)
    rank = jnp.sum(onef * (before + carry[:, 0:1]), axis=0, keepdims=True)
    carry[...] = carry[...] + jnp.sum(onef, axis=1, keepdims=True)
    bucket_ref[0] = bucket
    rank_ref[0] = rank.astype(jnp.int32)
    counts_ref[...] = carry[...]


def _moe_route(h, fn, wr, bias, *, tr=512):
    n, d = h.shape
    nt = n // tr
    tri = jnp.triu(jnp.ones((tr, tr), F32), 1).astype(BF16)
    const = lambda i, *_: (0, 0)
    bucket, rank, counts = pl.pallas_call(
        _route_kernel,
        out_shape=(jax.ShapeDtypeStruct((nt, 1, tr), jnp.int32),
                   jax.ShapeDtypeStruct((nt, 1, tr), jnp.int32),
                   jax.ShapeDtypeStruct((BUCKET_ROWS, LANES), F32)),
        grid_spec=pltpu.PrefetchScalarGridSpec(
            num_scalar_prefetch=1,
            grid=(nt,),
            in_specs=[
                pl.BlockSpec((tr, d), lambda i, *_: (i, 0)),
                pl.BlockSpec((1, d), const),
                pl.BlockSpec((d, 2 * LANES), const),
                pl.BlockSpec((tr, tr), const),
            ],
            out_specs=[
                pl.BlockSpec((1, 1, tr), lambda i, *_: (i, 0, 0)),
                pl.BlockSpec((1, 1, tr), lambda i, *_: (i, 0, 0)),
                pl.BlockSpec((BUCKET_ROWS, LANES), const),
            ],
            scratch_shapes=[pltpu.VMEM((BUCKET_ROWS, LANES), F32)],
        ),
        compiler_params=_cparams(("arbitrary",)),
        name="moe_route",
    )(bias, h, fn, wr, tri)
    return bucket, rank, counts


SUBLANES = 8
PAD_PIECES = (256, 128, 64, 32, 16, 8)


def _scatter_kernel(pad_ref, dest_ref, src_ref, dst_ref, zeros, sem, zsem, *, ch, tail_pieces):
    first_step = pl.program_id(0) == 0

    def pad_copies():
        for b in range(N_BUCKETS):
            start, end = pad_ref[b], pad_ref[BUCKET_ROWS + b]
            aligned = (start + (SUBLANES - 1)) & ~(SUBLANES - 1)
            for k in range(SUBLANES - 1):
                yield start + k < aligned, pltpu.make_async_copy(
                    zeros.at[pl.ds(0, 1)], dst_ref.at[pl.ds(start + k, 1)], zsem)
            length = end - aligned
            for size in PAD_PIECES:
                row = pl.multiple_of(aligned + (length & ~(2 * size - 1)), SUBLANES)
                yield (length & size) != 0, pltpu.make_async_copy(
                    zeros.at[pl.ds(0, size)], dst_ref.at[pl.ds(row, size)], zsem)
        size = PAD_PIECES[0]
        for k in range(tail_pieces):
            row = pl.multiple_of(pad_ref[N_BUCKETS] + k * size, size)
            yield row < dst_ref.shape[0], pltpu.make_async_copy(
                zeros.at[pl.ds(0, size)], dst_ref.at[pl.ds(row, size)], zsem)

    @pl.when(first_step)
    def _():
        zeros[...] = jnp.zeros_like(zeros)
        for needed, copy in pad_copies():
            pl.when(needed)(copy.start)

    for j in range(ch):
        pltpu.make_async_copy(src_ref.at[pl.ds(j, 1)], dst_ref.at[pl.ds(dest_ref[0, 0, j], 1)],
                              sem).start(priority=j % 2)
    pltpu.make_async_copy(src_ref, dst_ref.at[pl.ds(0, ch)], sem).wait()

    @pl.when(first_step)
    def _():
        for needed, copy in pad_copies():
            pl.when(needed)(copy.wait)


def _gather_kernel(dest_ref, src_ref, dst_ref, sem, *, ch):
    for j in range(ch):
        pltpu.make_async_copy(src_ref.at[pl.ds(dest_ref[0, 0, j], 1)], dst_ref.at[pl.ds(j, 1)],
                              sem).start(priority=j % 2)
    pltpu.make_async_copy(src_ref.at[pl.ds(0, ch)], dst_ref, sem).wait()


def _scatter_rows(src, dest, pad, *, n_out, max_tail, ch=1024):
    n, d = src.shape
    return pl.pallas_call(
        functools.partial(_scatter_kernel, ch=ch, tail_pieces=pl.cdiv(max_tail, PAD_PIECES[0])),
        out_shape=jax.ShapeDtypeStruct((n_out, d), src.dtype),
        grid_spec=pltpu.PrefetchScalarGridSpec(
            num_scalar_prefetch=1,
            grid=(n // ch,),
            in_specs=[pl.BlockSpec((1, 1, ch), lambda i, *_: (i, 0, 0), memory_space=pltpu.SMEM),
                      pl.BlockSpec((ch, d), lambda i, *_: (i, 0))],
            out_specs=pl.BlockSpec(memory_space=pl.ANY),
            scratch_shapes=[pltpu.VMEM((PAD_PIECES[0], d), src.dtype), pltpu.SemaphoreType.DMA(()),
                            pltpu.SemaphoreType.DMA(())],
        ),
        compiler_params=_cparams(("arbitrary",)),
        name="moe_scatter",
    )(pad, dest.reshape(n // ch, 1, ch), src)


def _gather_rows(src, dest, *, ch=1024):
    n = dest.size
    d = src.shape[1]
    return pl.pallas_call(
        functools.partial(_gather_kernel, ch=ch),
        out_shape=jax.ShapeDtypeStruct((n, d), src.dtype),
        grid=(n // ch,),
        in_specs=[pl.BlockSpec((1, 1, ch), lambda i: (i, 0, 0), memory_space=pltpu.SMEM),
                  pl.BlockSpec(memory_space=pl.ANY)],
        out_specs=pl.BlockSpec((ch, d), lambda i: (i, 0)),
        scratch_shapes=[pltpu.SemaphoreType.DMA(())],
        compiler_params=_cparams(("arbitrary",)),
        name="moe_gather",
    )(dest.reshape(n // ch, 1, ch), src)


def _ffn_kernel(ea_ref, eb_ref, nv_ref, src_ref, hs_ref, fn_ref, wr_ref, w1a, w3a, w2a, w1b, w3b, w2b, o_ref):
    del src_ref
    i = pl.program_id(0)

    @pl.when(nv_ref[i] == 0)
    def _():
        o_ref[...] = jnp.zeros_like(o_ref)

    @pl.when(nv_ref[i] > 0)
    def _():
        x = hs_ref[...]
        xn = _rms(x, fn_ref[...])
        logits, xb = _router_logits(xn, wr_ref)
        score = jax.nn.sigmoid(logits)
        lane = lax.broadcasted_iota(jnp.int32, score.shape, 1)
        sa = jnp.sum(jnp.where(lane == ea_ref[i], score, 0.0), axis=-1, keepdims=True)
        sb = jnp.sum(jnp.where(lane == eb_ref[i], score, 0.0), axis=-1, keepdims=True)
        tot = sa + sb

        def expert(w1, w3, w2, gate):
            h1 = jnp.dot(xb, w1[...], preferred_element_type=F32)
            h3 = jnp.dot(xb, w3[...], preferred_element_type=F32)
            act = (h1 * jax.nn.sigmoid(h1)) * h3 * gate
            return jnp.dot(act.astype(BF16), w2[...], preferred_element_type=F32)

        o_ref[...] = x + expert(w1a, w3a, w2a, sa / tot) + expert(w1b, w3b, w2b, sb / tot)


def _moe_ffn(hs, fn, wr, w1, w3, w2, tile_ea, tile_eb, tile_nv, tile_src, *, tm):
    ns, d = hs.shape
    de = w1.shape[2]
    ntiles = ns // tm
    const = lambda i, *_: (0, 0)
    wa = lambda i, ea, eb, nv, src: (ea[i], 0, 0)
    wb = lambda i, ea, eb, nv, src: (eb[i], 0, 0)
    return pl.pallas_call(
        _ffn_kernel,
        out_shape=jax.ShapeDtypeStruct((ns, d), F32),
        grid_spec=pltpu.PrefetchScalarGridSpec(
            num_scalar_prefetch=4,
            grid=(ntiles,),
            in_specs=[
                pl.BlockSpec((tm, d), lambda i, ea, eb, nv, src: (src[i], 0)),
                pl.BlockSpec((1, d), const),
                pl.BlockSpec((d, 2 * LANES), const),
                pl.BlockSpec((None, d, de), wa),
                pl.BlockSpec((None, d, de), wa),
                pl.BlockSpec((None, de, d), wa),
                pl.BlockSpec((None, d, de), wb),
                pl.BlockSpec((None, d, de), wb),
                pl.BlockSpec((None, de, d), wb),
            ],
            out_specs=pl.BlockSpec((tm, d), lambda i, *_: (i, 0)),
        ),
        compiler_params=_cparams(("arbitrary",)),
        name="moe_ffn",
    )(tile_ea, tile_eb, tile_nv, tile_src, hs, fn, wr, w1, w3, w2, w1, w3, w2)


def _router_weights(router_w):
    d = router_w.shape[0]
    hi, lo = _split_bf16(router_w)
    wr = jnp.zeros((d, 2 * LANES), BF16)
    return wr.at[:, :N_EXPERTS].set(hi).at[:, LANES:LANES + N_EXPERTS].set(lo)


def _moe_layer(h, fn, wr, bias, w1, w3, w2, *, tm=512):
    assert tm <= 2 * PAD_PIECES[0]
    n, d = h.shape
    fn = fn.reshape(1, d)
    bucket, rank, counts = _moe_route(h, fn, wr, bias)
    counts = counts[:N_BUCKETS, 0].astype(jnp.int32)
    ntile_b = (counts + tm - 1) // tm
    tile_end = jnp.cumsum(ntile_b)
    tile_start = tile_end - ntile_b
    offs = jnp.zeros((BUCKET_ROWS,), jnp.int32).at[:N_BUCKETS].set(tile_start * tm)
    ntiles = n // tm + N_BUCKETS
    j = jnp.arange(ntiles, dtype=jnp.int32)
    used = j < tile_end[-1]
    tb = jnp.minimum(jnp.searchsorted(tile_end, j, side="right").astype(jnp.int32), N_BUCKETS - 1)
    last_b = jnp.max(jnp.where(ntile_b > 0, jnp.arange(N_BUCKETS, dtype=jnp.int32), 0))
    tb = jnp.where(used, tb, last_b)
    nv = jnp.where(used, jnp.clip(counts[tb] - (j - tile_start[tb]) * tm, 0, tm), 0).astype(jnp.int32)
    pair_lo = jnp.asarray([p[0] for p in PAIRS], jnp.int32)
    pair_hi = jnp.asarray([p[1] for p in PAIRS], jnp.int32)
    ea = GROUP_SIZE * (tb // len(PAIRS)) + pair_lo[tb % len(PAIRS)]
    eb = GROUP_SIZE * (tb // len(PAIRS)) + pair_hi[tb % len(PAIRS)]
    src_tile = jnp.where(used, j, tile_end[-1] - 1)
    dest = offs[bucket.reshape(n)] + rank.reshape(n)
    pad = jnp.zeros((2 * BUCKET_ROWS,), jnp.int32)
    pad = pad.at[:N_BUCKETS].set(tile_start * tm + counts).at[BUCKET_ROWS:BUCKET_ROWS + N_BUCKETS].set(
        tile_end * tm).at[N_BUCKETS].set(tile_end[-1] * tm)
    hs = _scatter_rows(h, dest, pad, n_out=ntiles * tm, max_tail=N_BUCKETS * tm)
    ys = _moe_ffn(hs, fn, wr, w1, w3, w2, ea, eb, nv, src_tile, tm=tm)
    return _gather_rows(ys, dest)


N_HEADS = 16
N_KV_GROUPS = 4
HEADS_PER_GROUP = N_HEADS // N_KV_GROUPS
HEAD_DIM = 64
CMP_LEN = 32
CMP_STRIDE = 16
SLC_LEN = 64
SLC_SHIFT = 6
SLC_TOP_N = 16
WINDOW = 512
N_BRANCH = 3
MASKED = -1e30
LOG2E = 1.4426950408889634
NT_DIMS = (((1,), (1,)), ((), ()))


def _segment_matrix():
    seg = np.kron(np.eye(LANES // HEAD_DIM), np.full((HEAD_DIM, HEAD_DIM), 1.0 / HEAD_DIM))
    return jnp.asarray(seg, BF16)


def _placement_matrix():
    p = np.zeros((LANES, 4 * LANES), np.float32)
    for half in range(2):
        for l in range(HEAD_DIM):
            p[half * HEAD_DIM + l, half * 2 * LANES + l] = 1.0
            p[half * HEAD_DIM + l, half * 2 * LANES + 3 * HEAD_DIM + l] = 1.0
    return jnp.asarray(p, BF16)


def _seg_rms(x, seg, gain):
    hi, lo = _split_bf16(x * x)
    ms = jnp.dot(hi, seg, preferred_element_type=F32) + jnp.dot(lo, seg, preferred_element_type=F32)
    return x * lax.rsqrt(ms + EPS) * gain


def _kv_proj_kernel(x_ref, kvn_ref, wkv_ref, seg_ref, pslab_ref, kgain_ref,
                    kcvc_ref, ks_ref, vs_ref, kw_ref, vw_ref):
    xn = _rms(x_ref[...], kvn_ref[...]).astype(BF16)
    kv = jnp.dot(xn, wkv_ref[...], preferred_element_type=F32)
    width = N_KV_GROUPS * HEAD_DIM
    for s in range(4):
        kcvc_ref[s] = kv[:, LANES * s:LANES * (s + 1)]
    seg = seg_ref[...]
    pslab = pslab_ref[...]

    def place(i, out_ref, gain_row):
        for s in range(width // LANES):
            slab = kv[:, width * i + LANES * s:width * i + LANES * (s + 1)]
            if gain_row is not None:
                slab = _seg_rms(slab, seg, kgain_ref[gain_row:gain_row + 1, :])
            out_ref[:, 4 * LANES * s:4 * LANES * (s + 1)] = jnp.dot(
                slab.astype(BF16), pslab, preferred_element_type=F32).astype(BF16)

    place(2, ks_ref, 1)
    place(3, vs_ref, None)
    place(4, kw_ref, 2)
    place(5, vw_ref, None)


def _kv_proj(x, kv_norm, w_kv, k_norm, *, t=512):
    n, d = x.shape
    width = N_KV_GROUPS * HEAD_DIM
    kgain = jnp.zeros((8, LANES), F32).at[:N_BRANCH].set(jnp.tile(k_norm, (1, LANES // HEAD_DIM)))
    const = lambda i: (0, 0)
    wide = jax.ShapeDtypeStruct((n, 4 * width), BF16)
    wide_spec = pl.BlockSpec((t, 4 * width), lambda i: (i, 0))
    return pl.pallas_call(
        _kv_proj_kernel,
        out_shape=(jax.ShapeDtypeStruct((4, n, LANES), F32), wide, wide, wide, wide),
        grid=(n // t,),
        in_specs=[
            pl.BlockSpec((t, d), lambda i: (i, 0)),
            pl.BlockSpec((1, d), const),
            pl.BlockSpec((d, 6 * width), const),
            pl.BlockSpec((LANES, LANES), const),
            pl.BlockSpec((LANES, 4 * LANES), const),
            pl.BlockSpec((8, LANES), const),
        ],
        out_specs=(pl.BlockSpec((4, t, LANES), lambda i: (0, i, 0)), wide_spec, wide_spec, wide_spec, wide_spec),
        compiler_params=_cparams(("arbitrary",)),
        name="nsa_kv_proj",
    )(x, kv_norm.reshape(1, d), w_kv.astype(BF16), _segment_matrix(), _placement_matrix(), kgain)


def _compress_kernel(x_ref, pe_ref, w1_ref, b1_ref, w2_ref, seg_ref, pslab_ref, gain_ref, o_ref, bm_scr,
                     *, normalize):
    seq = x_ref.shape[1]
    nc = seq // CMP_STRIDE
    halves = CMP_LEN // CMP_STRIDE
    assert halves == 2
    hidden = w1_ref.shape[2]
    for slab in range(2):
        first = jnp.zeros((nc, hidden), F32)
        second = jnp.zeros((nc, hidden), F32)
        for j in range(CMP_STRIDE):
            xj = x_ref[slab, pl.ds(j, nc, stride=CMP_STRIDE), :]
            first += jnp.dot((xj + pe_ref[j:j + 1, :]).astype(BF16), w1_ref[j], preferred_element_type=F32)
            second += jnp.dot((xj + pe_ref[CMP_STRIDE + j:CMP_STRIDE + j + 1, :]).astype(BF16),
                              w1_ref[CMP_STRIDE + j], preferred_element_type=F32)
        bm_scr[0:nc, :] = second
        bm_scr[nc:nc + 8, :] = jnp.zeros((8, hidden), F32)
        z = first + bm_scr[1:nc + 1, :] + b1_ref[...]
        hid = jax.nn.gelu(z)
        out = jnp.dot(hid.astype(BF16), w2_ref[...], preferred_element_type=F32)
        if normalize:
            out = _seg_rms(out, seg_ref[...], gain_ref[0:1, :])
        out = jnp.where(lax.broadcasted_iota(jnp.int32, out.shape, 0) < nc - 1, out, 0.0)
        o_ref[:, 4 * LANES * slab:4 * LANES * (slab + 1)] = jnp.dot(
            out.astype(BF16), pslab_ref[...], preferred_element_type=F32).astype(BF16)


def _compress(kcvc, which, pe, w1, b1, w2, gain, *, batch, normalize):
    n = kcvc.shape[1]
    seq = n // batch
    nc = seq // CMP_STRIDE
    hidden = w1.shape[1]
    eye2 = jnp.eye(2, dtype=F32)
    w1j = w1.reshape(CMP_LEN, HEAD_DIM, hidden)
    w1bd = jax.vmap(lambda m: jnp.kron(eye2, m))(w1j).astype(BF16)
    w2bd = jnp.kron(eye2, w2).astype(BF16)
    pe2 = jnp.tile(pe, (1, 2))
    b12 = jnp.tile(b1.reshape(1, hidden), (1, 2))
    gain2 = jnp.zeros((8, LANES), F32).at[0].set(jnp.tile(gain, 2))
    const2 = lambda b: (0, 0)
    return pl.pallas_call(
        functools.partial(_compress_kernel, normalize=normalize),
        out_shape=jax.ShapeDtypeStruct((batch, nc, 8 * LANES), BF16),
        grid=(batch,),
        in_specs=[
            pl.BlockSpec((2, seq, LANES), lambda b: (which, b, 0)),
            pl.BlockSpec((CMP_LEN, LANES), const2),
            pl.BlockSpec((CMP_LEN, LANES, 2 * hidden), lambda b: (0, 0, 0)),
            pl.BlockSpec((1, 2 * hidden), const2),
            pl.BlockSpec((2 * hidden, LANES), const2),
            pl.BlockSpec((LANES, LANES), const2),
            pl.BlockSpec((LANES, 4 * LANES), const2),
            pl.BlockSpec((8, LANES), const2),
        ],
        out_specs=pl.BlockSpec((None, nc, 8 * LANES), lambda b: (b, 0, 0)),
        scratch_shapes=[pltpu.VMEM((nc + 8, 2 * hidden), F32)],
        compiler_params=_cparams(("arbitrary",)),
        name="nsa_compress",
    )(kcvc, pe2, w1bd, b12, w2bd, _segment_matrix(), _placement_matrix(), gain2)


def _nsa_proj_kernel(h_ref, bn_ref, win_ref, seg_ref, qgain_ref, q_ref, g_ref):
    xn = _rms(h_ref[...], bn_ref[...]).astype(BF16)
    proj = jnp.dot(xn, win_ref[...], preferred_element_type=F32)
    nq = q_ref.shape[1]
    seg = seg_ref[...]
    for s in range(nq // LANES):
        slab = proj[:, LANES * s:LANES * (s + 1)]
        q_ref[:, LANES * s:LANES * (s + 1)] = _seg_rms(slab, seg, qgain_ref[...]).astype(BF16)
    g_ref[...] = jax.nn.sigmoid(proj[:, nq:nq + LANES])


def _nsa_proj(h, b_norm, w_in, q_norm, *, t=512):
    n, d = h.shape
    nq = N_HEADS * HEAD_DIM
    win = jnp.zeros((d, nq + LANES), BF16).at[:, :w_in.shape[1]].set(w_in.astype(BF16))
    qgain = jnp.tile(q_norm * (HEAD_DIM ** -0.5 * LOG2E), LANES // HEAD_DIM).reshape(1, LANES)
    const = lambda i: (0, 0)
    return pl.pallas_call(
        _nsa_proj_kernel,
        out_shape=(jax.ShapeDtypeStruct((n, nq), BF16), jax.ShapeDtypeStruct((n, LANES), F32)),
        grid=(n // t,),
        in_specs=[
            pl.BlockSpec((t, d), lambda i: (i, 0)),
            pl.BlockSpec((1, d), const),
            pl.BlockSpec((d, nq + LANES), const),
            pl.BlockSpec((LANES, LANES), const),
            pl.BlockSpec((1, LANES), const),
        ],
        out_specs=(pl.BlockSpec((t, nq), lambda i: (i, 0)), pl.BlockSpec((t, LANES), lambda i: (i, 0))),
        compiler_params=_cparams(("arbitrary",)),
        name="nsa_proj",
    )(h, b_norm.reshape(1, d), win, _segment_matrix(), qgain)


def _nsa_attn_kernel(cmask_ref, q_ref, g_ref, kc_ref, vc_ref, ks_ref, vs_ref, kw_ref, vw_ref, mapm_ref, gsel_ref,
                     o_ref, m_scr, acc_scr, *, tq):
    step = pl.program_id(2)
    q0 = step * tq
    seq = ks_ref.shape[0]
    rows = 2 * tq
    qq = jnp.concatenate([q_ref[:, 0:LANES], q_ref[:, LANES:2 * LANES]], axis=0)
    t = q0 + lax.broadcasted_iota(jnp.int32, (tq, 1), 0)
    t2 = jnp.concatenate([t, t], axis=0)
    lane_t = lax.broadcasted_iota(jnp.int32, (tq, LANES), 1)
    tk = 2 * tq
    lane_k = lax.broadcasted_iota(jnp.int32, (tk, LANES), 1)
    key_row = lax.broadcasted_iota(jnp.int32, (tk, LANES), 0)
    ones_half = (jnp.where(lane_k < HEAD_DIM, 1.0, 0.0).astype(BF16),
                 jnp.where(lane_k >= HEAD_DIM, 1.0, 0.0).astype(BF16))

    def reset():
        m_scr[...] = jnp.full(m_scr.shape, MASKED, F32)
        acc_scr[...] = jnp.zeros(acc_scr.shape, F32)

    def attend(lhs, k_blk, v_blk, valid, k_extra=None):
        chains = [(half, slice(pair * tq, (pair + 1) * tq)) for half in range(2) for pair in range(2)]
        probs = {}
        for half, r in chains:
            rhs = k_blk[:, LANES * half:LANES * (half + 1)]
            if k_extra is not None:
                rhs = jnp.concatenate([rhs, k_extra], axis=1)
            s = lax.dot_general(lhs[r], rhs, NT_DIMS, preferred_element_type=F32).astype(BF16)
            if valid is not None:
                s = jnp.where(valid, s, MASKED)
            m_old = m_scr[half, r, :]
            m_new = jnp.maximum(m_old, jnp.max(s, axis=-1, keepdims=True).astype(F32))
            m_scr[half, r, :] = m_new
            probs[half, r.start] = (jnp.exp2(s - m_new[:, 0:1].astype(BF16)), jnp.exp2(m_old - m_new))
        for half, r in chains:
            p, alpha = probs[half, r.start]
            v_aug = jnp.concatenate([v_blk[:, LANES * half:LANES * (half + 1)], ones_half[half]], axis=1)
            acc_scr[half, r, :] = (acc_scr[half, r, :] * jnp.concatenate([alpha, alpha], axis=1)
                                   + jnp.dot(p, v_aug, preferred_element_type=F32))

    def result():
        acc = acc_scr[0] + acc_scr[1]
        return acc[:, :LANES] / acc[:, LANES:]

    kc = kc_ref[...]
    vc = vc_ref[...]
    nc = kc.shape[0]
    cmp_end = lax.broadcasted_iota(jnp.int32, (1, nc), 1) * CMP_STRIDE + (CMP_LEN - 1)
    valid_c = cmp_end <= t2
    row_ok = t2 >= CMP_LEN - 1

    def cmp_probs(half):
        s = lax.dot_general(qq, kc[:, LANES * half:LANES * (half + 1)], NT_DIMS, preferred_element_type=F32)
        s = jnp.where(valid_c, s, MASKED)
        e = jnp.exp2(s - jnp.max(s, axis=-1, keepdims=True))
        return e * jnp.where(row_ok, 1.0 / jnp.sum(e, axis=-1, keepdims=True), 0.0)

    p_e = cmp_probs(0)
    p_o = cmp_probs(1)
    o_cmp = (jnp.dot(p_e.astype(BF16), vc[:, :LANES], preferred_element_type=F32)
             + jnp.dot(p_o.astype(BF16), vc[:, LANES:], preferred_element_type=F32))
    p_sum = (p_e[:tq] + p_o[:tq]) + (p_e[tq:] + p_o[tq:])
    hi, lo = _split_bf16(p_sum)
    mapm = mapm_ref[...]
    imp = jnp.dot(hi, mapm, preferred_element_type=F32) + jnp.dot(lo, mapm, preferred_element_type=F32)

    blk = lax.broadcasted_iota(jnp.int32, (LANES, tq), 0)
    blk_f = blk.astype(F32)
    cur = jnp.right_shift(q0 + lax.broadcasted_iota(jnp.int32, (1, tq), 1), SLC_SHIFT)
    forced = (blk == 0) | (blk == cur) | (blk == cur - 1)
    sel_t = jnp.where(forced, 1.0, 0.0)
    work = jnp.where(forced | (blk > cur), NEG_INF, imp.T)
    for _ in range(SLC_TOP_N - 3):
        m = jnp.max(work, axis=0, keepdims=True)
        idx = jnp.min(jnp.where(work == m, blk_f, float(LANES)), axis=0, keepdims=True)
        pick = blk_f == idx
        sel_t = jnp.where(pick, 1.0, sel_t)
        work = jnp.where(pick, NEG_INF, work)
    sel = sel_t.T

    unsel = jnp.where(sel > 0.5, 0.0, cmask_ref[0]).astype(BF16)
    lhs = jnp.concatenate([qq, jnp.concatenate([unsel, unsel], axis=0)], axis=1)
    reset()

    def key_tile(j, causal):
        k0 = pl.multiple_of(j * tk, tk)
        key_block = jnp.where(jnp.right_shift(k0 + key_row, SLC_SHIFT) == lane_k, -1.0, 0.0).astype(BF16)
        valid = (k0 + lax.broadcasted_iota(jnp.int32, (1, tk), 1)) <= t if causal else None
        attend(lhs, ks_ref[pl.ds(k0, tk), :], vs_ref[pl.ds(k0, tk), :], valid, k_extra=key_block)

    def full_tile(j, carry):
        key_tile(j, False)
        return carry

    last = (step + 2) // 2 - 1
    lax.fori_loop(0, last, full_tile, 0)
    key_tile(last, True)
    o_sel = result()

    reset()
    w0 = pl.multiple_of(jnp.clip(q0 - WINDOW, 0, seq - tk), tq)
    dist = t - (w0 + lax.broadcasted_iota(jnp.int32, (1, tk), 1))
    attend(qq, kw_ref[pl.ds(w0, tk), :], vw_ref[pl.ds(w0, tk), :], (dist >= 0) & (dist < WINDOW))
    o_win = result()

    gh, gl = _split_bf16(g_ref[...])
    gsel = gsel_ref[...]
    gx = jnp.dot(gh, gsel, preferred_element_type=F32) + jnp.dot(gl, gsel, preferred_element_type=F32)
    out = jnp.zeros((rows, LANES), F32)
    for br, o_br in enumerate((o_cmp, o_sel, o_win)):
        gate = jnp.concatenate([gx[:, LANES * br:LANES * (br + 1)],
                                gx[:, LANES * (N_BRANCH + br):LANES * (N_BRANCH + br + 1)]], axis=0)
        out = out + gate * o_br
    o_ref[:, 0:LANES] = out[:tq].astype(o_ref.dtype)
    o_ref[:, LANES:2 * LANES] = out[tq:].astype(o_ref.dtype)


def _importance_map(nc, n_cmp):
    ratio, span = SLC_LEN // CMP_STRIDE, CMP_LEN // CMP_STRIDE
    w = np.convolve(np.ones(ratio), np.ones(span))
    m = np.zeros((nc, LANES), np.float32)
    for j in range(LANES):
        for o, wt in enumerate(w):
            c = ratio * j + o - (span - 1)
            if 0 <= c < n_cmp:
                m[c, j] = wt
    return jnp.asarray(m, BF16)


def _gate_select():
    m = np.zeros((N_KV_GROUPS, LANES, 2 * N_BRANCH * LANES), np.float32)
    for g in range(N_KV_GROUPS):
        for pair in range(2):
            for br in range(N_BRANCH):
                for lane in range(LANES):
                    r = 2 * pair + lane // HEAD_DIM
                    m[g, (g * HEADS_PER_GROUP + r) * N_BRANCH + br, (pair * N_BRANCH + br) * LANES + lane] = 1.0
    return jnp.asarray(m, BF16)


def _nsa_attention(q, gates, shared, q_norm, k_norm, *, batch, tq=512):
    kcmp, vcmp, ks, vs, kw, vw = shared
    n = q.shape[0]
    seq = n // batch
    assert WINDOW == tq and seq % (2 * tq) == 0
    nq = seq // tq
    nc = kcmp.shape[1]
    width = 2 * LANES
    ks, vs, kw, vw = (a.reshape(batch, seq, N_KV_GROUPS * width) for a in (ks, vs, kw, vw))
    bound = 1.02 * LOG2E * HEAD_DIM ** 0.5 * jnp.max(jnp.abs(q_norm)) * jnp.max(jnp.abs(k_norm[1]))
    cmask = jnp.exp2(jnp.ceil(jnp.log2(2.0 * bound + 1000.0))).reshape(1).astype(F32)
    qrow = lambda b, g, i, *_: (b * nq + i, g)
    per_bg = lambda b, g, i, *_: (b, 0, g)
    return pl.pallas_call(
        functools.partial(_nsa_attn_kernel, tq=tq),
        out_shape=jax.ShapeDtypeStruct((n, N_HEADS * HEAD_DIM), BF16),
        grid_spec=pltpu.PrefetchScalarGridSpec(
            num_scalar_prefetch=1,
            grid=(batch, N_KV_GROUPS, nq),
            in_specs=[
                pl.BlockSpec((tq, width), qrow),
                pl.BlockSpec((tq, LANES), lambda b, g, i, *_: (b * nq + i, 0)),
                pl.BlockSpec((None, nc, width), per_bg),
                pl.BlockSpec((None, nc, width), per_bg),
                pl.BlockSpec((None, seq, width), per_bg),
                pl.BlockSpec((None, seq, width), per_bg),
                pl.BlockSpec((None, seq, width), per_bg),
                pl.BlockSpec((None, seq, width), per_bg),
                pl.BlockSpec((nc, LANES), lambda b, g, i, *_: (0, 0)),
                pl.BlockSpec((None, LANES, 2 * N_BRANCH * LANES), lambda b, g, i, *_: (g, 0, 0)),
            ],
            out_specs=pl.BlockSpec((tq, width), qrow),
            scratch_shapes=[pltpu.VMEM((2, 2 * tq, LANES), F32), pltpu.VMEM((2, 2 * tq, 2 * LANES), F32)],
        ),
        compiler_params=_cparams(("arbitrary", "arbitrary", "arbitrary")),
        name="nsa_attention",
    )(cmask, q, gates, kcmp, vcmp, ks, vs, kw, vw,
      _importance_map(nc, (seq - CMP_LEN) // CMP_STRIDE + 1), _gate_select())


def _out_proj_kernel(h_ref, o_ref, w_ref, out_ref):
    out_ref[...] = h_ref[...] + jnp.dot(o_ref[...], w_ref[...], preferred_element_type=F32)


def _out_proj(h, o, w_out, *, t=512):
    n, d = h.shape
    k = o.shape[1]
    return pl.pallas_call(
        _out_proj_kernel,
        out_shape=jax.ShapeDtypeStruct((n, d), F32),
        grid=(n // t,),
        in_specs=[pl.BlockSpec((t, d), lambda i: (i, 0)), pl.BlockSpec((t, k), lambda i: (i, 0)),
                  pl.BlockSpec((k, d), lambda i: (0, 0))],
        out_specs=pl.BlockSpec((t, d), lambda i: (i, 0)),
        compiler_params=_cparams(("arbitrary",)),
        name="nsa_out_proj",
    )(h, o, w_out.astype(BF16))


def _nsa_shared_kv(x, kv_norm, w_kv, cmp_pe, phi_w1, phi_b1, phi_w2, k_norm, *, batch):
    kcvc, ks, vs, kw, vw = _kv_proj(x, kv_norm, w_kv, k_norm)
    kcmp = _compress(kcvc, 0, cmp_pe[0], phi_w1[0], phi_b1[0], phi_w2[0], k_norm[0], batch=batch, normalize=True)
    vcmp = _compress(kcvc, 1, cmp_pe[1], phi_w1[1], phi_b1[1], phi_w2[1], k_norm[0], batch=batch, normalize=False)
    return kcmp, vcmp, ks, vs, kw, vw


def _nsa_layer(h, b_norm, w_in, q_norm, w_out, k_norm, shared, *, batch):
    q, gates = _nsa_proj(h, b_norm, w_in, q_norm)
    o = _nsa_attention(q, gates, shared, q_norm, k_norm, batch=batch)
    return _out_proj(h, o, w_out)


def kernel(x, a_norm, a_w_in, a_dw, a_dw_bias, a_conv_norm, a_w_out, kv_norm, w_kv, cmp_pe, phi_w1, phi_b1, phi_w2, k_norm, b_norm, b_w_in, b_q_norm, b_w_out, ffn_norm, router_w, router_bias, moe_w1, moe_w3, moe_w2):
    bsz, s, d = x.shape
    h = x.reshape(bsz * s, d)
    wr = _router_weights(router_w)
    n_a = a_norm.shape[0]
    depth = ffn_norm.shape[0]
    shared = None
    for layer in range(depth):
        if layer < n_a:
            i = layer
            h = _conv_layer(h, a_norm[i], a_w_in[i], a_dw[i], a_dw_bias[i], a_conv_norm[i], a_w_out[i], batch=bsz)
        else:
            i = layer - n_a
            h = _nsa_layer(h, b_norm[i], b_w_in[i], b_q_norm[i], b_w_out[i], k_norm, shared, batch=bsz)
        h = _moe_layer(h, ffn_norm[layer], wr, router_bias, moe_w1[layer].astype(BF16),
                       moe_w3[layer].astype(BF16), moe_w2[layer].astype(BF16))
        if layer == n_a - 1:
            shared = _nsa_shared_kv(h, kv_norm, w_kv, cmp_pe, phi_w1, phi_b1, phi_w2, k_norm, batch=bsz)
    return h.reshape(bsz, s, d)
```
